```python
import jax
import jax.numpy as jnp
from jax import lax
import numpy as np

D_MODEL = 1024
BATCH = 16
SEQ = 4096
DEPTH = 1

N_MEM = 256
RMS_EPS = 1e-6

RW_HEADS = 8
RW_HEAD_DIM = 64
RW_WIDTH = RW_HEADS * RW_HEAD_DIM
RW_DECAY_LORA = 64
RW_AAA_LORA = 64
RW_GATE_LORA = 128
RW_GN_EPS = 64e-5

DSA_HEADS = 8
DSA_HEAD_DIM = 64
DSA_WIDTH = DSA_HEADS * DSA_HEAD_DIM
DSA_KV_RANK = 128
IDX_HEADS = 4
IDX_DIM = 64
IDX_TOPK_MAX = 256
Q_BLOCK = 128

X_HEADS = 4
X_HEAD_DIM = 256
X_WIDTH = X_HEADS * X_HEAD_DIM

N_GROUPS = 4
EXPERTS_PER_GROUP = 8
N_EXPERTS = N_GROUPS * EXPERTS_PER_GROUP
TOP_K_INNER = 2
D_EXPERT = 512
MOE_BLOCK = 256

RW_SIZES = (RW_WIDTH, RW_WIDTH, RW_WIDTH, RW_DECAY_LORA, RW_AAA_LORA, RW_GATE_LORA)
DSA_SIZES = (DSA_WIDTH, DSA_KV_RANK, IDX_HEADS * IDX_DIM, IDX_DIM, IDX_HEADS)
RW_COLS = sum(RW_SIZES)
DSA_COLS = sum(DSA_SIZES)
GATE_COLS = 2 * D_MODEL
IN_COLS = RW_COLS + DSA_COLS + GATE_COLS

kernel_name = 'hybrid_rwkv7_dsa_hmoe_block'


def _split(t, sizes):
    cuts = [int(c) for c in np.cumsum(sizes)[:-1]]
    return jnp.split(t, cuts, axis=-1)


def rmsnorm(x, g):
    xf = x.astype(jnp.float32)
    y = xf * lax.rsqrt(jnp.mean(xf * xf, axis=-1, keepdims=True) + RMS_EPS)
    return (y * g.astype(jnp.float32)).astype(x.dtype)


def rwkv7_branch(z, mu, w0, w2, a0, a2, g2, k_k, k_a, r_k, ln_w, ln_b):
    B, S, _ = z.shape
    H, N = RW_HEADS, RW_HEAD_DIM
    f32 = jnp.float32
    z_prev = jnp.pad(z, ((0, 0), (1, 0), (0, 0)))[:, :-1]
    z = z + (z_prev - z) * mu
    r, k, v, xw, xa, xg = _split(z, RW_SIZES)
    w = -jax.nn.softplus(-(w0 + jnp.tanh(xw) @ w2).astype(f32)) - 0.5
    decay = jnp.exp(-jnp.exp(w))
    a = jax.nn.sigmoid(a0 + xa @ a2)
    g = jax.nn.sigmoid(xg) @ g2
    heads = lambda t: t.astype(f32).reshape(B, S, H, N)
    kk = heads(k * k_k)
    kk = kk / jnp.maximum(jnp.sqrt(jnp.sum(kk * kk, -1, keepdims=True)), 1e-12)
    k = k * (1.0 + (a - 1.0) * k_a)
    r_h, k_h, v_h, a_h = heads(r), heads(k), heads(v), heads(a)
    seq_major = lambda t: jnp.moveaxis(t, 1, 0)
    xs = (seq_major(r_h), seq_major(heads(decay)), seq_major(k_h),
          seq_major(v_h), seq_major(kk), seq_major(a_h))

    def step(state, inp):
        r_t, w_t, k_t, v_t, kk_t, a_t = inp
        sa = jnp.einsum('bhij,bhj->bhi', state, -kk_t)
        state = (state * w_t[:, :, None, :]
                 + sa[..., None] * (kk_t * a_t)[:, :, None, :]
                 + v_t[..., None] * k_t[:, :, None, :])
        return state, jnp.einsum('bhij,bhj->bhi', state, r_t)

    _, y = lax.scan(step, jnp.zeros((B, H, N, N), f32), xs)
    y = jnp.moveaxis(y, 0, 1)
    mean = jnp.mean(y, -1, keepdims=True)
    var = jnp.mean(jnp.square(y - mean), -1, keepdims=True)
    y = ((y - mean) * lax.rsqrt(var + RW_GN_EPS)).reshape(B, S, RW_WIDTH)
    y = y * ln_w.astype(f32) + ln_b.astype(f32)
    bonus = jnp.sum(r_h * k_h * r_k.astype(f32), -1, keepdims=True) * v_h
    y = y + bonus.reshape(B, S, RW_WIDTH)
    return (y * g.astype(f32)).astype(z.dtype)


def dsa_branch(z, kv_norm, w_uk, w_uv):
    B, S, _ = z.shape
    f32 = jnp.float32
    k_sel = min(IDX_TOPK_MAX, S // 4)
    q, c_kv, q_idx, k_idx, w_idx = _split(z, DSA_SIZES)
    q = q.reshape(B, S, DSA_HEADS, DSA_HEAD_DIM)
    c_kv = rmsnorm(c_kv, kv_norm)
    q_lat = jnp.einsum('bshd,rhd->bshr', q, w_uk) * (DSA_HEAD_DIM ** -0.5)
    q_idx = q_idx.reshape(B, S, IDX_HEADS, IDX_DIM)
    w_idx = w_idx * ((IDX_HEADS * IDX_DIM) ** -0.5)
    nb = S // Q_BLOCK
    to_blocks = lambda t: jnp.moveaxis(t.reshape(B, nb, Q_BLOCK, *t.shape[2:]), 1, 0)
    key_pos = jnp.arange(S)

    def block(args):
        b, ql, qi, wi = args
        t_pos = b * Q_BLOCK + jnp.arange(Q_BLOCK)
        rel = jax.nn.relu(jnp.einsum('bqhd,bsd->bqhs', qi, k_idx).astype(f32))
        iscore = jnp.einsum('bqh,bqhs->bqs', wi.astype(f32), rel)
        causal = key_pos[None, :] <= t_pos[:, None]
        iscore = jnp.where(causal[None], iscore, -jnp.inf)
        _, idx = lax.top_k(iscore, k_sel)
        valid = idx <= t_pos[None, :, None]
        c_sel = jax.vmap(lambda c, i: c[i])(c_kv, idx)
        s = jnp.einsum('bqhr,bqkr->bhqk', ql, c_sel).astype(f32)
        s = jnp.where(valid[:, None], s, -jnp.inf)
        p = jax.nn.softmax(s, axis=-1).astype(c_sel.dtype)
        o_lat = jnp.einsum('bhqk,bqkr->bqhr', p, c_sel)
        return jnp.einsum('bqhr,rhd->bqhd', o_lat, w_uv)

    o = lax.map(block, (jnp.arange(nb), to_blocks(q_lat), to_blocks(q_idx), to_blocks(w_idx)))
    return jnp.moveaxis(o, 0, 1).reshape(B, S, DSA_WIDTH)


def cross_attention(hn, mem_n, w_cq, w_ckv, w_co):
    B, S, _ = hn.shape
    M = mem_n.shape[1]
    q = (hn @ w_cq).reshape(B, S, X_HEADS, X_HEAD_DIM)
    k, v = jnp.split((mem_n @ w_ckv).reshape(B, M, 2, X_HEADS, X_HEAD_DIM), 2, axis=2)
    k, v = k[:, :, 0], v[:, :, 0]
    s = jnp.einsum('bshd,bmhd->bhsm', q, k).astype(jnp.float32) * (X_HEAD_DIM ** -0.5)
    p = jax.nn.softmax(s, axis=-1).astype(v.dtype)
    o = jnp.einsum('bhsm,bmhd->bshd', p, v).reshape(B, S, X_WIDTH)
    return o @ w_co


def hier_moe(hn, w_rg, b_rg, w_re, b_re, w_gate, w_up, w_down):
    B, S, D = hn.shape
    T = B * S
    xt = hn.reshape(T, D)
    g_prob = jax.nn.softmax((xt @ w_rg + b_rg).astype(jnp.float32), axis=-1)
    p_grp, grp = lax.top_k(g_prob, 1)
    e_logits = (xt @ w_re + b_re).astype(jnp.float32).reshape(T, N_GROUPS, EXPERTS_PER_GROUP)
    e_logits = jnp.take_along_axis(e_logits, grp[:, :, None], axis=1)[:, 0]
    p_e, e_loc = lax.top_k(jax.nn.softmax(e_logits, axis=-1), TOP_K_INNER)
    gate = p_grp * p_e / jnp.sum(p_e, -1, keepdims=True)
    expert = grp * EXPERTS_PER_GROUP + e_loc
    A = T * TOP_K_INNER
    e_flat = expert.reshape(A)
    tok_flat = jnp.repeat(jnp.arange(T, dtype=jnp.int32), TOP_K_INNER)
    g_flat = gate.reshape(A)
    order = jnp.argsort(e_flat)
    e_sorted = e_flat[order]
    counts = jnp.bincount(e_flat, length=N_EXPERTS)
    start = jnp.cumsum(counts) - counts
    padded = (counts + MOE_BLOCK - 1) // MOE_BLOCK * MOE_BLOCK
    pend = jnp.cumsum(padded)
    pstart = pend - padded
    dest = pstart[e_sorted] + (jnp.arange(A) - start[e_sorted])
    P = A + N_EXPERTS * MOE_BLOCK
    buf_tok = jnp.full((P,), T, jnp.int32).at[dest].set(tok_flat[order])
    buf_gate = jnp.zeros((P,), jnp.float32).at[dest].set(g_flat[order])
    n_blocks = P // MOE_BLOCK
    blk_expert = jnp.minimum(
        jnp.searchsorted(pend, jnp.arange(n_blocks) * MOE_BLOCK, side='right'), N_EXPERTS - 1)
    x_pad = jnp.concatenate([xt, jnp.zeros((1, D), xt.dtype)], axis=0)

    def run_block(args):
        e, toks, gts = args
        xb = x_pad[toks]
        hid = jax.nn.silu(xb @ w_gate[e]) * (xb @ w_up[e])
        return (hid @ w_down[e]) * gts[:, None].astype(xb.dtype)

    yb = lax.map(run_block, (blk_expert, buf_tok.reshape(n_blocks, MOE_BLOCK),
                             buf_gate.reshape(n_blocks, MOE_BLOCK)))
    y = jnp.zeros((T + 1, D), hn.dtype).at[buf_tok].add(yb.reshape(P, D))[:T]
    return y.reshape(B, S, D)


def hybrid_layer(h, mem, norm_mix, w_in, shift_mu, rw_w0, rw_w2, rw_a0, rw_a2, rw_g2,
                 rw_k_k, rw_k_a, rw_r_k, rw_ln_w, rw_ln_b, kv_norm, w_uk, w_uv,
                 w_proj_a, w_proj_b, b_gate, w_out, norm_cross, norm_mem, w_cq, w_ckv, w_co,
                 norm_ffn, w_router_g, b_router_g, w_router_e, b_router_e,
                 w_e_gate, w_e_up, w_e_down):
    z = rmsnorm(h, norm_mix) @ w_in
    z_rw, z_dsa, z_gate = _split(z, (RW_COLS, DSA_COLS, GATE_COLS))
    y_a = rwkv7_branch(z_rw, shift_mu, rw_w0, rw_w2, rw_a0, rw_a2, rw_g2,
                       rw_k_k, rw_k_a, rw_r_k, rw_ln_w, rw_ln_b)
    y_b = dsa_branch(z_dsa, kv_norm, w_uk, w_uv)
    g_a, g_b = jnp.split(jax.nn.sigmoid(z_gate + b_gate), 2, axis=-1)
    h = h + (g_a * (y_a @ w_proj_a) + g_b * (y_b @ w_proj_b)) @ w_out
    h = h + cross_attention(rmsnorm(h, norm_cross), rmsnorm(mem, norm_mem), w_cq, w_ckv, w_co)
    h = h + hier_moe(rmsnorm(h, norm_ffn), w_router_g, b_router_g, w_router_e, b_router_e,
                     w_e_gate, w_e_up, w_e_down)
    return h


def setup_inputs(seed: int = 0) -> dict:
    key = jax.random.key(seed)
    ks = iter(jax.random.split(key, 48))
    f32 = jnp.float32
    L = DEPTH

    def nrm(shape, scale):
        return jax.random.normal(next(ks), shape, f32) * scale

    def gain(shape):
        return 1.0 + nrm(shape, 0.02)

    return {
        'x': nrm((BATCH, SEQ, D_MODEL), 1.0),
        'mem': nrm((BATCH, N_MEM, D_MODEL), 1.0),
        'norm_mix': gain((L, D_MODEL)),
        'w_in': nrm((L, D_MODEL, IN_COLS), D_MODEL ** -0.5),
        'shift_mu': jax.random.uniform(next(ks), (L, RW_COLS), f32),
        'rw_w0': jax.random.uniform(next(ks), (L, RW_WIDTH), f32, -5.0, 1.0),
        'rw_w2': nrm((L, RW_DECAY_LORA, RW_WIDTH), 0.1 * RW_DECAY_LORA ** -0.5),
        'rw_a0': nrm((L, RW_WIDTH), 0.1),
        'rw_a2': nrm((L, RW_AAA_LORA, RW_WIDTH), 0.1 * RW_AAA_LORA ** -0.5),
        'rw_g2': nrm((L, RW_GATE_LORA, RW_WIDTH), RW_GATE_LORA ** -0.5),
        'rw_k_k': 0.85 + nrm((L, RW_WIDTH), 0.02),
        'rw_k_a': gain((L, RW_WIDTH)),
        'rw_r_k': nrm((L, RW_HEADS, RW_HEAD_DIM), 0.1),
        'rw_ln_w': gain((L, RW_WIDTH)),
        'rw_ln_b': nrm((L, RW_WIDTH), 0.01),
        'kv_norm': gain((L, DSA_KV_RANK)),
        'w_uk': nrm((L, DSA_KV_RANK, DSA_HEADS, DSA_HEAD_DIM), DSA_KV_RANK ** -0.5),
        'w_uv': nrm((L, DSA_KV_RANK, DSA_HEADS, DSA_HEAD_DIM), DSA_KV_RANK ** -0.5),
        'w_proj_a': nrm((L, RW_WIDTH, D_MODEL), RW_WIDTH ** -0.5),
        'w_proj_b': nrm((L, DSA_WIDTH, D_MODEL), DSA_WIDTH ** -0.5),
        'b_gate': nrm((L, GATE_COLS), 0.01),
        'w_out': nrm((L, D_MODEL, D_MODEL), D_MODEL ** -0.5),
        'norm_cross': gain((L, D_MODEL)),
        'norm_mem': gain((L, D_MODEL)),
        'w_cq': nrm((L, D_MODEL, X_WIDTH), D_MODEL ** -0.5),
        'w_ckv': nrm((L, D_MODEL, 2 * X_WIDTH), D_MODEL ** -0.5),
        'w_co': nrm((L, X_WIDTH, D_MODEL), X_WIDTH ** -0.5),
        'norm_ffn': gain((L, D_MODEL)),
        'w_router_g': nrm((L, D_MODEL, N_GROUPS), D_MODEL ** -0.5),
        'b_router_g': nrm((L, N_GROUPS), 0.01),
        'w_router_e': nrm((L, D_MODEL, N_EXPERTS), D_MODEL ** -0.5),
        'b_router_e': nrm((L, N_EXPERTS), 0.01),
        'w_e_gate': nrm((L, N_EXPERTS, D_MODEL, D_EXPERT), D_MODEL ** -0.5),
        'w_e_up': nrm((L, N_EXPERTS, D_MODEL, D_EXPERT), D_MODEL ** -0.5),
        'w_e_down': nrm((L, N_EXPERTS, D_EXPERT, D_MODEL), D_EXPERT ** -0.5),
        'norm_final': gain((D_MODEL,)),
    }


def reference(x, mem, norm_mix, w_in, shift_mu, rw_w0, rw_w2, rw_a0, rw_a2, rw_g2,
              rw_k_k, rw_k_a, rw_r_k, rw_ln_w, rw_ln_b, kv_norm, w_uk, w_uv,
              w_proj_a, w_proj_b, b_gate, w_out, norm_cross, norm_mem, w_cq, w_ckv, w_co,
              norm_ffn, w_router_g, b_router_g, w_router_e, b_router_e,
              w_e_gate, w_e_up, w_e_down, norm_final):
    h = x
    for l in range(DEPTH):
        h = hybrid_layer(h, mem, norm_mix[l], w_in[l], shift_mu[l], rw_w0[l], rw_w2[l],
                         rw_a0[l], rw_a2[l], rw_g2[l], rw_k_k[l], rw_k_a[l], rw_r_k[l],
                         rw_ln_w[l], rw_ln_b[l], kv_norm[l], w_uk[l], w_uv[l],
                         w_proj_a[l], w_proj_b[l], b_gate[l], w_out[l],
                         norm_cross[l], norm_mem[l], w_cq[l], w_ckv[l], w_co[l],
                         norm_ffn[l], w_router_g[l], b_router_g[l], w_router_e[l], b_router_e[l],
                         w_e_gate[l], w_e_up[l], w_e_down[l])
    return rmsnorm(h, norm_final)
```

```python
import functools

import jax
import jax.numpy as jnp
from jax import lax
from jax.experimental import pallas as pl
from jax.experimental.pallas import tpu as pltpu

F32 = jnp.float32
BF16 = jnp.bfloat16
I32 = jnp.int32

RMS_EPS = 1e-6
RW_HEADS = 8
RW_HEAD_DIM = 64
RW_WIDTH = RW_HEADS * RW_HEAD_DIM
RW_DECAY_LORA = 64
RW_AAA_LORA = 64
RW_GATE_LORA = 128
RW_GN_EPS = 64e-5
DSA_HEADS = 8
DSA_HEAD_DIM = 64
DSA_WIDTH = DSA_HEADS * DSA_HEAD_DIM
DSA_KV_RANK = 128
IDX_HEADS = 4
IDX_DIM = 64
IDX_TOPK_MAX = 256
X_HEADS = 4
N_GROUPS = 4
EXPERTS_PER_GROUP = 8
N_EXPERTS = N_GROUPS * EXPERTS_PER_GROUP
MOE_BLOCK = 256

LANE = 128
RW_CHUNK = 64
RW_ZCOLS = 3 * RW_WIDTH + 3 * LANE
DSA_ZCOLS = DSA_WIDTH + DSA_KV_RANK + IDX_HEADS * LANE + LANE + LANE
DSA_TQ = 256
ROW_CH = 512
NEG = -1e30
INT_MIN = -(2 ** 31)
VMEM_LIMIT = 56 * 1024 * 1024


def _mm(a, b):
    return jnp.dot(a.astype(BF16), b.astype(BF16), preferred_element_type=F32)


def _mm_nt(a, b):
    return lax.dot_general(a.astype(BF16), b.astype(BF16), (((1,), (1,)), ((), ())),
                           preferred_element_type=F32)


def _mm_tn(a, b):
    return lax.dot_general(a.astype(BF16), b.astype(BF16), (((0,), (0,)), ((), ())),
                           preferred_element_type=F32)


def _split3(x):
    hi = x.astype(BF16)
    r1 = x - hi.astype(F32)
    mid = r1.astype(BF16)
    lo = (r1 - mid.astype(F32)).astype(BF16)
    return hi, mid, lo


def _mm_exact_rhs(x, b):
    hi, mid, lo = _split3(x)
    return (jnp.dot(hi, b, preferred_element_type=F32) + jnp.dot(mid, b, preferred_element_type=F32)
            + jnp.dot(lo, b, preferred_element_type=F32))


def _mm_exact_lhs(a, x):
    hi, mid, lo = _split3(x)
    return (jnp.dot(a, hi, preferred_element_type=F32) + jnp.dot(a, mid, preferred_element_type=F32)
            + jnp.dot(a, lo, preferred_element_type=F32))


def _rms(x, g):
    return x * lax.rsqrt(jnp.mean(x * x, axis=-1, keepdims=True) + RMS_EPS) * g


def _sigmoid(x):
    return 1.0 / (1.0 + jnp.exp(-x))


def _softplus(x):
    return jnp.maximum(x, 0.0) + jnp.log1p(jnp.exp(-jnp.abs(x)))


def _cparams(sem):
    return pltpu.CompilerParams(dimension_semantics=sem, vmem_limit_bytes=VMEM_LIMIT)


def _const_spec(shape):
    nd = len(shape)
    return pl.BlockSpec(shape, lambda *_: (0,) * nd)


def _inproj_kernel(x_ref, g_ref, wrw_ref, wdsa_ref, wgate_ref, bgate_ref, wuk_ref, kvn_ref,
                   zrw_ref, qlat_ref, ckv_ref, qidx_ref, kidx_ref, widx_ref, gate_ref):
    hn = _rms(x_ref[...], g_ref[...]).astype(BF16)
    zrw_ref[...] = jnp.dot(hn, wrw_ref[...], preferred_element_type=F32)
    zd = jnp.dot(hn, wdsa_ref[...], preferred_element_type=F32)
    o = 0
    q = zd[:, o:o + DSA_WIDTH]; o += DSA_WIDTH
    c = zd[:, o:o + DSA_KV_RANK]; o += DSA_KV_RANK
    qi = zd[:, o:o + IDX_HEADS * LANE]; o += IDX_HEADS * LANE
    ki = zd[:, o:o + LANE]; o += LANE
    wi = zd[:, o:o + LANE]
    qlat_ref[...] = (_mm(q, wuk_ref[...]) * (DSA_HEAD_DIM ** -0.5)).astype(BF16)
    ckv_ref[...] = _rms(c, kvn_ref[...]).astype(BF16)
    qidx_ref[...] = qi.astype(BF16)
    kidx_ref[...] = ki.astype(BF16)
    widx_ref[...] = wi * ((IDX_HEADS * IDX_DIM) ** -0.5)
    zg = jnp.dot(hn, wgate_ref[...], preferred_element_type=F32) + bgate_ref[...]
    gate_ref[...] = _sigmoid(zg)


def _inproj(x2, norm_mix, w_rw, w_dsa, w_gate, b_gate, wuk_bd, kv_norm, tm):
    T, D = x2.shape
    grid = (T // tm,)
    row = lambda n: pl.BlockSpec((tm, n), lambda i: (i, 0))
    outs = [
        jax.ShapeDtypeStruct((T, RW_ZCOLS), F32),
        jax.ShapeDtypeStruct((T, DSA_HEADS * DSA_KV_RANK), BF16),
        jax.ShapeDtypeStruct((T, DSA_KV_RANK), BF16),
        jax.ShapeDtypeStruct((T, IDX_HEADS * LANE), BF16),
        jax.ShapeDtypeStruct((T, LANE), BF16),
        jax.ShapeDtypeStruct((T, LANE), F32),
        jax.ShapeDtypeStruct((T, 2 * D), F32),
    ]
    return pl.pallas_call(
        _inproj_kernel,
        grid=grid,
        in_specs=[row(D), _const_spec((1, D)), _const_spec(w_rw.shape), _const_spec(w_dsa.shape),
                  _const_spec(w_gate.shape), _const_spec((1, 2 * D)), _const_spec(wuk_bd.shape),
                  _const_spec((1, DSA_KV_RANK))],
        out_specs=[row(s.shape[1]) for s in outs],
        out_shape=outs,
        compiler_params=_cparams(("parallel",)),
        name="inproj",
    )(x2, norm_mix.reshape(1, D), w_rw, w_dsa, w_gate, b_gate.reshape(1, 2 * D), wuk_bd,
      kv_norm.reshape(1, DSA_KV_RANK))


def _rwkv_kernel(z_ref, mu_ref, w0_ref, w2_ref, a0_ref, a2_ref, g2_ref, kk_ref, ka_ref, rk_ref,
                 lnw_ref, lnb_ref, hs_ref, y_ref, state_ref, carry_ref, ybuf_ref):
    C, N, H = RW_CHUNK, RW_HEAD_DIM, RW_HEADS
    W = RW_WIDTH

    @pl.when(pl.program_id(1) == 0)
    def _():
        state_ref[...] = jnp.zeros_like(state_ref)
        carry_ref[...] = jnp.zeros_like(carry_ref)

    z = z_ref[...]
    row = lax.broadcasted_iota(I32, z.shape, 0)
    zprev = jnp.where(row == 0, carry_ref[...], pltpu.roll(z, 1, 0))
    carry_ref[...] = z[C - 1:C, :]
    zm = z + (zprev - z) * mu_ref[...]
    r = zm[:, 0:W]
    k = zm[:, W:2 * W]
    v = zm[:, 2 * W:3 * W]
    xw = zm[:, 3 * W:3 * W + LANE]
    xa = zm[:, 3 * W + LANE:3 * W + 2 * LANE]
    xg = zm[:, 3 * W + 2 * LANE:3 * W + 3 * LANE]

    hs = hs_ref[...]
    headsum = lambda t: _mm_exact_rhs(t, hs)

    wlog = -_softplus(-(w0_ref[...] + _mm(jnp.tanh(xw), w2_ref[...]))) - 0.5
    dlog = -jnp.exp(wlog)
    a = _sigmoid(a0_ref[...] + _mm(xa, a2_ref[...]))
    g = _mm(_sigmoid(xg), g2_ref[...])
    kk = k * kk_ref[...]
    kk = kk / jnp.maximum(jnp.sqrt(headsum(kk * kk)), 1e-12)
    k_eff = k * (1.0 + (a - 1.0) * ka_ref[...])
    bonus = headsum(r * k_eff * rk_ref[...]) * v

    ti = lax.broadcasted_iota(I32, (C, C), 0)
    si = lax.broadcasted_iota(I32, (C, C), 1)
    ltri = (si <= ti).astype(BF16)
    cum = _mm_exact_lhs(ltri, dlog)
    pin = jnp.exp(cum)
    pex = jnp.exp(cum - dlog)
    pinv = jnp.exp(-cum)
    a_t = -kk * pex
    r_t = r * pin
    b_t = kk * a * pinv
    k_t = k_eff * pinv
    p_c = pin[C - 1:C, :]

    t2 = lax.broadcasted_iota(I32, (C, 2 * C), 0)
    s2 = lax.broadcasted_iota(I32, (C, 2 * C), 1) % C
    m_strict = s2 < t2
    m_incl = s2 <= t2
    eye = (si == ti).astype(F32)
    zeros_cn = jnp.zeros((C, N), F32)

    for h in range(H):
        sl = slice(h * N, (h + 1) * N)
        ar = jnp.concatenate([a_t[:, sl], r_t[:, sl]], axis=0)
        bk = jnp.concatenate([b_t[:, sl], k_t[:, sl]], axis=0)
        q = _mm_nt(ar, bk)
        top = jnp.where(m_strict, q[:C], 0.0)
        bot = jnp.where(m_incl, q[C:], 0.0)
        s0 = state_ref[h]
        w_all = _mm_nt(ar, s0)
        vh = v[:, sl]
        rhs = w_all[:C] + _mm(top, jnp.concatenate([zeros_cn, vh], axis=0))
        a_ab = top[:, :C]
        inv = eye + a_ab
        pw = a_ab
        for _ in range(5):
            pw = _mm(pw, pw)
            inv = inv + _mm(pw, inv)
        u = _mm(inv, rhs)
        gsrc = jnp.concatenate([u, vh], axis=0)
        ybuf_ref[:, sl] = w_all[C:] + _mm(bot, gsrc)
        state_ref[h] = (s0 + _mm_tn(gsrc, bk)) * p_c[:, sl]

    y = ybuf_ref[...]
    mean = headsum(y) * (1.0 / N)
    d = y - mean
    var = headsum(d * d) * (1.0 / N)
    yn = d * lax.rsqrt(var + RW_GN_EPS) * lnw_ref[...] + lnb_ref[...]
    y_ref[...] = ((yn + bonus) * g).astype(y_ref.dtype)


def _rwkv(z_rw, B, S, mu, w0, w2, a0, a2, g2, k_k, k_a, r_k, ln_w, ln_b):
    C, W = RW_CHUNK, RW_WIDTH
    nc = S // C
    hid = jnp.arange(W) // RW_HEAD_DIM
    hs = (hid[:, None] == hid[None, :]).astype(BF16)
    vec = lambda t: t.reshape(1, -1)
    consts = [vec(mu), vec(w0), w2, vec(a0), a2, g2, vec(k_k), vec(k_a), vec(r_k), vec(ln_w),
              vec(ln_b), hs]
    return pl.pallas_call(
        _rwkv_kernel,
        grid=(B, nc),
        in_specs=[pl.BlockSpec((C, RW_ZCOLS), lambda b, c: (b * nc + c, 0))]
                 + [_const_spec(t.shape) for t in consts],
        out_specs=pl.BlockSpec((C, W), lambda b, c: (b * nc + c, 0)),
        out_shape=jax.ShapeDtypeStruct((B * S, W), BF16),
        scratch_shapes=[pltpu.VMEM((RW_HEADS, RW_HEAD_DIM, RW_HEAD_DIM), F32),
                        pltpu.VMEM((1, RW_ZCOLS), F32),
                        pltpu.VMEM((C, W), F32)],
        compiler_params=_cparams(("parallel", "arbitrary")),
        name="rwkv",
    )(z_rw, *consts)


def _dsa_kernel(qlat_ref, qidx_ref, widx_ref, ckv_ref, kidx_ref, wuv_ref, y_ref,
                keys_ref, thr_ref, need_ref, eqc_ref, m_ref, l_ref, acc_ref, *, k_sel):
    TQ = DSA_TQ
    RC = 64
    qi = pl.program_id(1)
    nkt = qi + 1
    rowpos = qi * TQ + lax.broadcasted_iota(I32, (TQ, TQ), 0)
    lane = lax.broadcasted_iota(I32, (TQ, TQ), 1)

    def score_body(kt, carry):
        kx = kidx_ref[pl.ds(pl.multiple_of(kt * TQ, TQ), TQ), :]
        sc = jnp.zeros((TQ, TQ), F32)
        for h in range(IDX_HEADS):
            rel = jnp.maximum(_mm_nt(qidx_ref[:, h * LANE:(h + 1) * LANE], kx), 0.0)
            sc = sc + widx_ref[:, h:h + 1] * rel
        sc = jnp.where(sc == 0.0, 0.0, sc)
        bits = pltpu.bitcast(sc, I32)
        key = jnp.where(bits >= 0, bits, bits ^ jnp.int32(0x7FFFFFFF))
        keys_ref[kt] = jnp.where(kt * TQ + lane <= rowpos, key, jnp.int32(INT_MIN))
        return carry

    lax.fori_loop(0, nkt, score_body, 0)

    def count(rc, pred):
        def body(kt, acc):
            hit = pred(keys_ref[kt, rc * RC:(rc + 1) * RC, :]).astype(I32)
            return acc + hit[:, :LANE] + hit[:, LANE:]
        acc = lax.fori_loop(0, nkt, body, jnp.zeros((RC, LANE), I32))
        return jnp.sum(acc, axis=-1, keepdims=True)

    for rc in range(TQ // RC):
        def bit_body(i, tu):
            cand_u = tu | lax.shift_left(jnp.int32(1), 31 - i)
            cand = cand_u ^ jnp.int32(INT_MIN)
            cnt = count(rc, lambda kk: kk >= cand)
            return jnp.where(cnt >= k_sel, cand_u, tu)
        tu = lax.fori_loop(0, 32, bit_body, jnp.zeros((RC, 1), I32))
        thr = tu ^ jnp.int32(INT_MIN)
        n_gt = count(rc, lambda kk: kk > thr)
        thr_ref[rc * RC:(rc + 1) * RC, :] = thr
        need_ref[rc * RC:(rc + 1) * RC, :] = (k_sel - n_gt).astype(F32)

    eqc_ref[...] = jnp.zeros_like(eqc_ref)
    m_ref[...] = jnp.full_like(m_ref, NEG)
    l_ref[...] = jnp.zeros_like(l_ref)
    acc_ref[...] = jnp.zeros_like(acc_ref)
    ustrict = (lax.broadcasted_iota(I32, (TQ, TQ), 0) < lane).astype(BF16)

    def att_body(kt, carry):
        key = keys_ref[kt]
        thr = thr_ref[...]
        eq = key == thr
        eqf = eq.astype(F32)
        pre = jnp.dot(eqf.astype(BF16), ustrict, preferred_element_type=F32) + eqc_ref[...]
        sel = ((key > thr) | (eq & (pre < need_ref[...]))) & (kt * TQ + lane <= rowpos)
        bias = jnp.where(sel, 0.0, NEG)
        eqc_ref[...] += jnp.sum(eqf, axis=-1, keepdims=True)
        ct = ckv_ref[pl.ds(pl.multiple_of(kt * TQ, TQ), TQ), :]
        for h in range(DSA_HEADS):
            s = _mm_nt(qlat_ref[:, h * DSA_KV_RANK:(h + 1) * DSA_KV_RANK], ct) + bias
            m_old = m_ref[h]
            m_new = jnp.maximum(m_old, jnp.max(s, axis=-1, keepdims=True))
            p = jnp.exp(s - m_new)
            alpha = jnp.exp(m_old - m_new)
            l_ref[h] = alpha * l_ref[h] + jnp.sum(p, axis=-1, keepdims=True)
            acc_ref[h] = alpha * acc_ref[h] + jnp.dot(p.astype(BF16), ct, preferred_element_type=F32)
            m_ref[h] = m_new
        return carry

    lax.fori_loop(0, nkt, att_body, 0)
    o = jnp.concatenate([(acc_ref[h] / l_ref[h]).astype(BF16) for h in range(DSA_HEADS)], axis=-1)
    y_ref[...] = jnp.dot(o, wuv_ref[...], preferred_element_type=F32).astype(y_ref.dtype)


def _dsa(qlat, qidx, widx, ckv, kidx, wuv_bd, B, S):
    TQ = DSA_TQ
    nq = S // TQ
    k_sel = min(IDX_TOPK_MAX, S // 4)
    qrow = lambda n: pl.BlockSpec((TQ, n), lambda b, q: (b * nq + q, 0))
    brow = lambda n: pl.BlockSpec((S, n), lambda b, q: (b, 0))
    return pl.pallas_call(
        functools.partial(_dsa_kernel, k_sel=k_sel),
        grid=(B, nq),
        in_specs=[qrow(DSA_HEADS * DSA_KV_RANK), qrow(IDX_HEADS * LANE), qrow(LANE),
                  brow(DSA_KV_RANK), brow(LANE), _const_spec(wuv_bd.shape)],
        out_specs=qrow(DSA_WIDTH),
        out_shape=jax.ShapeDtypeStruct((B * S, DSA_WIDTH), BF16),
        scratch_shapes=[pltpu.VMEM((nq, TQ, TQ), I32),
                        pltpu.VMEM((TQ, 1), I32), pltpu.VMEM((TQ, 1), F32), pltpu.VMEM((TQ, 1), F32),
                        pltpu.VMEM((DSA_HEADS, TQ, 1), F32), pltpu.VMEM((DSA_HEADS, TQ, 1), F32),
                        pltpu.VMEM((DSA_HEADS, TQ, DSA_KV_RANK), F32)],
        compiler_params=_cparams(("parallel", "arbitrary")),
        name="dsa",
    )(qlat, qidx, widx, ckv, kidx, wuv_bd)


def _memkv_kernel(m_ref, g_ref, w_ref, o_ref):
    o_ref[...] = _mm(_rms(m_ref[...], g_ref[...]), w_ref[...]).astype(o_ref.dtype)


def _memkv(mem2, norm_mem, w_ckv, M):
    R, D = mem2.shape
    return pl.pallas_call(
        _memkv_kernel,
        grid=(R // M,),
        in_specs=[pl.BlockSpec((M, D), lambda i: (i, 0)), _const_spec((1, D)), _const_spec(w_ckv.shape)],
        out_specs=pl.BlockSpec((M, w_ckv.shape[1]), lambda i: (i, 0)),
        out_shape=jax.ShapeDtypeStruct((R, w_ckv.shape[1]), BF16),
        compiler_params=_cparams(("parallel",)),
        name="memkv",
    )(mem2, norm_mem.reshape(1, D), w_ckv)


def _mix_kernel(x_ref, ya_ref, yb_ref, gate_ref, kv_ref, wpa_ref, wpb_ref, wout_ref, ncross_ref,
                wcq_ref, wco_ref, nffn_ref, wrh_ref, wrl_ref, br_ref,
                h2_ref, hn3_ref, ids_ref, gts_ref):
    D = x_ref.shape[1]
    xw = D // X_HEADS
    a = jnp.dot(ya_ref[...], wpa_ref[...], preferred_element_type=F32)
    b = jnp.dot(yb_ref[...], wpb_ref[...], preferred_element_type=F32)
    mix = gate_ref[:, :D] * a + gate_ref[:, D:] * b
    h1 = x_ref[...] + _mm(mix, wout_ref[...])
    q = _mm(_rms(h1, ncross_ref[...]), wcq_ref[...]).astype(BF16)
    outs = []
    for h in range(X_HEADS):
        s = _mm_nt(q[:, h * xw:(h + 1) * xw], kv_ref[:, h * xw:(h + 1) * xw]) * (xw ** -0.5)
        e = jnp.exp(s - jnp.max(s, axis=-1, keepdims=True))
        p = e / jnp.sum(e, axis=-1, keepdims=True)
        outs.append(_mm(p, kv_ref[:, D + h * xw:D + (h + 1) * xw]).astype(BF16))
    h2 = h1 + jnp.dot(jnp.concatenate(outs, axis=-1), wco_ref[...], preferred_element_type=F32)
    h2_ref[...] = h2
    hn3 = _rms(h2, nffn_ref[...])
    hn3_ref[...] = hn3

    hi = hn3.astype(BF16)
    lo = (hn3 - hi.astype(F32)).astype(BF16)
    lg = (jnp.dot(hi, wrh_ref[...], preferred_element_type=F32)
          + jnp.dot(lo, wrh_ref[...], preferred_element_type=F32)
          + jnp.dot(hi, wrl_ref[...], preferred_element_type=F32)) + br_ref[...]
    lane = lax.broadcasted_iota(I32, lg.shape, 1)
    big = jnp.int32(1 << 20)
    rmax = lambda t: jnp.max(t, axis=-1, keepdims=True)
    rsum = lambda t: jnp.sum(t, axis=-1, keepdims=True)
    first = lambda hit: jnp.min(jnp.where(hit, lane, big), axis=-1, keepdims=True)
    gmask = lane < N_GROUPS
    gl = jnp.where(gmask, lg, NEG)
    ge = jnp.where(gmask, jnp.exp(gl - rmax(gl)), 0.0)
    gp = jnp.where(gmask, ge / rsum(ge), -1.0)
    p_grp = rmax(gp)
    grp = first(gp == p_grp)
    eidx = lane - N_GROUPS
    emask = (eidx >= 0) & (eidx < N_EXPERTS) & ((eidx // EXPERTS_PER_GROUP) == grp)
    el = jnp.where(emask, lg, NEG)
    ee = jnp.where(emask, jnp.exp(el - rmax(el)), 0.0)
    ep = jnp.where(emask, ee / rsum(ee), -1.0)
    p1 = rmax(ep)
    i1 = first(ep == p1)
    ep2 = jnp.where(lane == i1, -1.0, ep)
    p2 = rmax(ep2)
    i2 = first(ep2 == p2)
    den = p1 + p2
    ids_ref[...] = jnp.where(lane == 0, i1 - N_GROUPS, jnp.where(lane == 1, i2 - N_GROUPS, 0))
    gts_ref[...] = jnp.where(lane == 0, p_grp * p1 / den, jnp.where(lane == 1, p_grp * p2 / den, 0.0))


def _mix(x2, ya, yb, gates, kv, wpa, wpb, wout, ncross, wcq, wco, nffn, wr_hi, wr_lo, br, B, S, M, tm):
    T, D = x2.shape
    nt = S // tm
    row = lambda n: pl.BlockSpec((tm, n), lambda i: (i, 0))
    consts = [wpa, wpb, wout, ncross.reshape(1, D), wcq, wco, nffn.reshape(1, D), wr_hi, wr_lo, br]
    outs = [jax.ShapeDtypeStruct((T, D), F32), jax.ShapeDtypeStruct((T, D), F32),
            jax.ShapeDtypeStruct((T, LANE), I32), jax.ShapeDtypeStruct((T, LANE), F32)]
    return pl.pallas_call(
        _mix_kernel,
        grid=(T // tm,),
        in_specs=[row(D), row(RW_WIDTH), row(DSA_WIDTH), row(2 * D),
                  pl.BlockSpec((M, 2 * D), lambda i: (i // nt, 0))]
                 + [_const_spec(t.shape) for t in consts],
        out_specs=[row(D), row(D), row(LANE), row(LANE)],
        out_shape=outs,
        compiler_params=_cparams(("parallel",)),
        name="mix_cross_router",
    )(x2, ya, yb, gates, kv, *consts)


def _rowcopy_kernel(*refs, scatter):
    if scatter:
        slot_ref, src_ref, _, out_ref, sem = refs
    else:
        slot_ref, src_ref, out_ref, sem = refs
    base = pl.program_id(0) * ROW_CH

    def copy(i):
        s = slot_ref[0, 0, i]
        a = base + i
        if scatter:
            return pltpu.make_async_copy(src_ref.at[a // 2], out_ref.at[s], sem)
        return pltpu.make_async_copy(src_ref.at[s], out_ref.at[a], sem)

    def start(i, c):
        copy(i).start()
        return c

    def wait(i, c):
        copy(i).wait()
        return c

    lax.fori_loop(0, ROW_CH, start, 0)
    lax.fori_loop(0, ROW_CH, wait, 0)


def _rowcopy(slots, src, out_rows, scatter):
    A = slots.shape[0]
    D = src.shape[-1]
    slots3 = slots.reshape(A // ROW_CH, 1, ROW_CH)
    anyspec = pl.BlockSpec(memory_space=pl.ANY)
    args = [slots3, src]
    in_specs = [pl.BlockSpec((1, 1, ROW_CH), lambda i: (i, 0, 0), memory_space=pltpu.SMEM), anyspec]
    aliases = {}
    if scatter:
        args.append(jnp.zeros((out_rows, 1, D), src.dtype))
        in_specs.append(anyspec)
        aliases = {2: 0}
    return pl.pallas_call(
        functools.partial(_rowcopy_kernel, scatter=scatter),
        grid=(A // ROW_CH,),
        in_specs=in_specs,
        out_specs=anyspec,
        out_shape=jax.ShapeDtypeStruct((out_rows, 1, D), src.dtype),
        scratch_shapes=[pltpu.SemaphoreType.DMA(())],
        input_output_aliases=aliases,
        compiler_params=_cparams(("arbitrary",)),
        name="moe_scatter" if scatter else "moe_gather",
    )(*args)


def _expert_kernel(be_ref, nu_ref, x_ref, wg_ref, wu_ref, wd_ref, o_ref):
    i = pl.program_id(0)

    @pl.when(i < nu_ref[0])
    def _():
        x = x_ref[...].astype(BF16)
        gt = jnp.dot(x, wg_ref[...], preferred_element_type=F32)
        up = jnp.dot(x, wu_ref[...], preferred_element_type=F32)
        hid = gt * _sigmoid(gt) * up
        o_ref[...] = _mm(hid, wd_ref[...])

    @pl.when(i >= nu_ref[0])
    def _():
        o_ref[...] = jnp.zeros_like(o_ref)


def _experts(xs, blk_expert, n_used, w_gate, w_up, w_down):
    P, D = xs.shape
    E, _, F = w_gate.shape
    nb = P // MOE_BLOCK
    grid_spec = pltpu.PrefetchScalarGridSpec(
        num_scalar_prefetch=2,
        grid=(nb,),
        in_specs=[pl.BlockSpec((MOE_BLOCK, D), lambda i, be, nu: (i, 0)),
                  pl.BlockSpec((None, D, F), lambda i, be, nu: (be[i], 0, 0)),
                  pl.BlockSpec((None, D, F), lambda i, be, nu: (be[i], 0, 0)),
                  pl.BlockSpec((None, F, D), lambda i, be, nu: (be[i], 0, 0))],
        out_specs=pl.BlockSpec((MOE_BLOCK, D), lambda i, be, nu: (i, 0)),
    )
    return pl.pallas_call(
        _expert_kernel,
        grid_spec=grid_spec,
        out_shape=jax.ShapeDtypeStruct((P, D), F32),
        compiler_params=_cparams(("arbitrary",)),
        name="experts",
    )(blk_expert, n_used, xs, w_gate, w_up, w_down)


def _final_kernel(h_ref, yg_ref, gts_ref, g_ref, o_ref, *, apply_norm):
    D = h_ref.shape[1]
    y = gts_ref[:, 0:1] * yg_ref[:, :D] + gts_ref[:, 1:2] * yg_ref[:, D:]
    h = h_ref[...] + y
    o_ref[...] = _rms(h, g_ref[...]) if apply_norm else h


def _final(h2, yg, gts, norm_final, apply_norm, tm):
    T, D = h2.shape
    row = lambda n: pl.BlockSpec((tm, n), lambda i: (i, 0))
    return pl.pallas_call(
        functools.partial(_final_kernel, apply_norm=apply_norm),
        grid=(T // tm,),
        in_specs=[row(D), row(2 * D), row(LANE), _const_spec((1, D))],
        out_specs=row(D),
        out_shape=jax.ShapeDtypeStruct((T, D), F32),
        compiler_params=_cparams(("parallel",)),
        name="final",
    )(h2, yg, gts, norm_final.reshape(1, D))


def _pad_cols(w, n):
    return jnp.pad(w, ((0, 0), (0, n - w.shape[1])))


def _pad_rows(w, n):
    return jnp.pad(w, ((0, n - w.shape[0]), (0, 0)))


def _layer(h, mem, norm_mix, w_in, shift_mu, rw_w0, rw_w2, rw_a0, rw_a2, rw_g2, rw_k_k, rw_k_a,
           rw_r_k, rw_ln_w, rw_ln_b, kv_norm, w_uk, w_uv, w_proj_a, w_proj_b, b_gate, w_out,
           norm_cross, norm_mem, w_cq, w_ckv, w_co, norm_ffn, w_router_g, b_router_g, w_router_e,
           b_router_e, w_e_gate, w_e_up, w_e_down):
    B, S, D = h.shape
    M = mem.shape[1]
    T = B * S
    W = RW_WIDTH
    x2 = h.reshape(T, D)

    c = 0
    pieces = {}
    for name, n in (("r", W), ("k", W), ("v", W), ("xw", RW_DECAY_LORA), ("xa", RW_AAA_LORA),
                    ("xg", RW_GATE_LORA), ("q", DSA_WIDTH), ("ckv", DSA_KV_RANK),
                    ("qidx", IDX_HEADS * IDX_DIM), ("kidx", IDX_DIM), ("widx", IDX_HEADS),
                    ("gate", 2 * D)):
        pieces[name] = (c, n)
        c += n
    col = lambda name: w_in[:, pieces[name][0]:pieces[name][0] + pieces[name][1]]
    mu_of = lambda name: shift_mu[pieces[name][0]:pieces[name][0] + pieces[name][1]]
    w_rw = jnp.concatenate([col("r"), col("k"), col("v"), _pad_cols(col("xw"), LANE),
                            _pad_cols(col("xa"), LANE), _pad_cols(col("xg"), LANE)], axis=1).astype(BF16)
    padv = lambda t: jnp.pad(t, (0, LANE - t.shape[0]))
    mu = jnp.concatenate([mu_of("r"), mu_of("k"), mu_of("v"), padv(mu_of("xw")), padv(mu_of("xa")),
                          padv(mu_of("xg"))])
    qidx_w = col("qidx").reshape(D, IDX_HEADS, IDX_DIM)
    qidx_w = jnp.pad(qidx_w, ((0, 0), (0, 0), (0, LANE - IDX_DIM))).reshape(D, IDX_HEADS * LANE)
    w_dsa = jnp.concatenate([col("q"), col("ckv"), qidx_w, _pad_cols(col("kidx"), LANE),
                             _pad_cols(col("widx"), LANE)], axis=1).astype(BF16)
    w_gate_in = col("gate").astype(BF16)
    eye_h = jnp.eye(DSA_HEADS, dtype=F32)
    wuk_bd = jnp.einsum("rhd,hg->hdgr", w_uk, eye_h).reshape(DSA_WIDTH, DSA_HEADS * DSA_KV_RANK).astype(BF16)
    wuv_bd = jnp.einsum("rhd,hg->hrgd", w_uv, eye_h).reshape(DSA_HEADS * DSA_KV_RANK, DSA_WIDTH).astype(BF16)

    tm1 = min(256, T)
    z_rw, qlat, ckv, qidx, kidx, widx, gates = _inproj(
        x2, norm_mix, w_rw, w_dsa, w_gate_in, b_gate, wuk_bd, kv_norm, tm1)

    y_a = _rwkv(z_rw, B, S, mu, rw_w0, _pad_rows(rw_w2, LANE).astype(BF16), rw_a0,
                _pad_rows(rw_a2, LANE).astype(BF16), rw_g2.astype(BF16), rw_k_k, rw_k_a, rw_r_k,
                rw_ln_w, rw_ln_b)
    y_b = _dsa(qlat, qidx, widx, ckv, kidx, wuv_bd, B, S)
    kv = _memkv(mem.reshape(B * M, D), norm_mem, w_ckv.astype(BF16), M)

    w_r = jnp.concatenate([w_router_g, w_router_e], axis=1)
    w_r = _pad_cols(w_r, LANE)
    wr_hi = w_r.astype(BF16)
    wr_lo = (w_r - wr_hi.astype(F32)).astype(BF16)
    b_r = jnp.pad(jnp.concatenate([b_router_g, b_router_e]), (0, LANE - N_GROUPS - N_EXPERTS)).reshape(1, LANE)
    tm5 = min(256, S)
    h2, hn3, ids, gts = _mix(x2, y_a, y_b, gates, kv, w_proj_a.astype(BF16), w_proj_b.astype(BF16),
                             w_out.astype(BF16), norm_cross, w_cq.astype(BF16), w_co.astype(BF16),
                             norm_ffn, wr_hi, wr_lo, b_r, B, S, M, tm5)

    A = 2 * T
    e_flat = ids[:, :2].reshape(A)
    onehot = (e_flat[:, None] == jnp.arange(N_EXPERTS, dtype=I32)[None, :]).astype(I32)
    csum = jnp.cumsum(onehot, axis=0)
    rank = jnp.sum(csum * onehot, axis=1) - 1
    counts = csum[-1]
    padded = (counts + MOE_BLOCK - 1) // MOE_BLOCK * MOE_BLOCK
    pend = jnp.cumsum(padded)
    pstart = pend - padded
    slots = (jnp.sum(pstart[None, :] * onehot, axis=1) + rank).astype(I32)
    P = A + N_EXPERTS * MOE_BLOCK
    nb = P // MOE_BLOCK
    blk_expert = jnp.minimum(
        jnp.searchsorted(pend, jnp.arange(nb, dtype=I32) * MOE_BLOCK, side="right"), N_EXPERTS - 1).astype(I32)
    n_used = (pend[-1:] // MOE_BLOCK).astype(I32)

    xs = _rowcopy(slots, hn3.reshape(T, 1, D), P, scatter=True)
    ys = _experts(xs.reshape(P, D), blk_expert, n_used, w_e_gate.astype(BF16), w_e_up.astype(BF16),
                  w_e_down.astype(BF16))
    yg = _rowcopy(slots, ys.reshape(P, 1, D), A, scatter=False)
    return h2, yg.reshape(T, 2 * D), gts


def kernel(x, mem, norm_mix, w_in, shift_mu, rw_w0, rw_w2, rw_a0, rw_a2, rw_g2, rw_k_k, rw_k_a, rw_r_k, rw_ln_w, rw_ln_b, kv_norm, w_uk, w_uv, w_proj_a, w_proj_b, b_gate, w_out, norm_cross, norm_mem, w_cq, w_ckv, w_co, norm_ffn, w_router_g, b_router_g, w_router_e, b_router_e, w_e_gate, w_e_up, w_e_down, norm_final):
    B, S, D = x.shape
    depth = norm_mix.shape[0]
    h = x
    for l in range(depth):
        last = l == depth - 1
        h2, yg, gts = _layer(
            h, mem, norm_mix[l], w_in[l], shift_mu[l], rw_w0[l], rw_w2[l], rw_a0[l], rw_a2[l],
            rw_g2[l], rw_k_k[l], rw_k_a[l], rw_r_k[l], rw_ln_w[l], rw_ln_b[l], kv_norm[l], w_uk[l],
            w_uv[l], w_proj_a[l], w_proj_b[l], b_gate[l], w_out[l], norm_cross[l], norm_mem[l],
            w_cq[l], w_ckv[l], w_co[l], norm_ffn[l], w_router_g[l], b_router_g[l], w_router_e[l],
            b_router_e[l], w_e_gate[l], w_e_up[l], w_e_down[l])
        h = _final(h2, yg, gts, norm_final, last, min(256, B * S)).reshape(B, S, D)
    return h
```

```python
import functools

import jax
import jax.numpy as jnp
from jax import lax
from jax.experimental import pallas as pl
from jax.experimental.pallas import tpu as pltpu

F32 = jnp.float32
BF16 = jnp.bfloat16
I32 = jnp.int32

RMS_EPS = 1e-6
RW_HEADS = 8
RW_HEAD_DIM = 64
RW_WIDTH = RW_HEADS * RW_HEAD_DIM
RW_DECAY_LORA = 64
RW_AAA_LORA = 64
RW_GATE_LORA = 128
RW_GN_EPS = 64e-5
DSA_HEADS = 8
DSA_HEAD_DIM = 64
DSA_WIDTH = DSA_HEADS * DSA_HEAD_DIM
DSA_KV_RANK = 128
IDX_HEADS = 4
IDX_DIM = 64
IDX_TOPK_MAX = 256
X_HEADS = 4
N_GROUPS = 4
EXPERTS_PER_GROUP = 8
N_EXPERTS = N_GROUPS * EXPERTS_PER_GROUP
MOE_BLOCK = 256

LANE = 128
RW_CHUNK = 64
RW_ZCOLS = 3 * RW_WIDTH + 3 * LANE
DSA_ZCOLS = DSA_WIDTH + DSA_KV_RANK + IDX_HEADS * LANE + LANE + LANE
DSA_TQ = 256
ROW_TOK = 256
NEG = -1e30
INT_MIN = -(2 ** 31)
VMEM_LIMIT = 56 * 1024 * 1024


def _mm(a, b):
    return jnp.dot(a.astype(BF16), b.astype(BF16), preferred_element_type=F32)


def _mm_nt(a, b):
    return lax.dot_general(a.astype(BF16), b.astype(BF16), (((1,), (1,)), ((), ())),
                           preferred_element_type=F32)


def _mm_tn(a, b):
    return lax.dot_general(a.astype(BF16), b.astype(BF16), (((0,), (0,)), ((), ())),
                           preferred_element_type=F32)


def _split3(x):
    hi = x.astype(BF16)
    r1 = x - hi.astype(F32)
    mid = r1.astype(BF16)
    lo = (r1 - mid.astype(F32)).astype(BF16)
    return hi, mid, lo


def _mm_exact_rhs(x, b):
    hi, mid, lo = _split3(x)
    return (jnp.dot(hi, b, preferred_element_type=F32) + jnp.dot(mid, b, preferred_element_type=F32)
            + jnp.dot(lo, b, preferred_element_type=F32))


def _mm_exact_lhs(a, x):
    hi, mid, lo = _split3(x)
    return (jnp.dot(a, hi, preferred_element_type=F32) + jnp.dot(a, mid, preferred_element_type=F32)
            + jnp.dot(a, lo, preferred_element_type=F32))


def _rms(x, g):
    return x * lax.rsqrt(jnp.mean(x * x, axis=-1, keepdims=True) + RMS_EPS) * g


def _sigmoid(x):
    return 1.0 / (1.0 + jnp.exp(-x))


def _softplus(x):
    return jnp.maximum(x, 0.0) + jnp.log1p(jnp.exp(-jnp.abs(x)))


def _cparams(sem):
    return pltpu.CompilerParams(dimension_semantics=sem, vmem_limit_bytes=VMEM_LIMIT)


def _const_spec(shape):
    nd = len(shape)
    return pl.BlockSpec(shape, lambda *_: (0,) * nd)


def _inproj_kernel(x_ref, g_ref, wrw_ref, wdsa_ref, wgate_ref, bgate_ref, wuk_ref, kvn_ref,
                   zrw_ref, qlat_ref, ckv_ref, qidx_ref, kidx_ref, widx_ref, gate_ref):
    hn = _rms(x_ref[...], g_ref[...]).astype(BF16)
    zrw_ref[...] = jnp.dot(hn, wrw_ref[...], preferred_element_type=F32)
    zd = jnp.dot(hn, wdsa_ref[...], preferred_element_type=F32)
    o = 0
    q = zd[:, o:o + DSA_WIDTH]; o += DSA_WIDTH
    c = zd[:, o:o + DSA_KV_RANK]; o += DSA_KV_RANK
    qi = zd[:, o:o + IDX_HEADS * LANE]; o += IDX_HEADS * LANE
    ki = zd[:, o:o + LANE]; o += LANE
    wi = zd[:, o:o + LANE]
    qlat_ref[...] = (_mm(q, wuk_ref[...]) * (DSA_HEAD_DIM ** -0.5)).astype(BF16)
    cn = _rms(c, kvn_ref[...]).astype(BF16)
    ckv_ref[...] = jnp.concatenate([cn, jnp.ones_like(cn)], axis=-1)
    qidx_ref[...] = qi.astype(BF16)
    kidx_ref[...] = ki.astype(BF16)
    widx_ref[...] = wi * ((IDX_HEADS * IDX_DIM) ** -0.5)
    zg = jnp.dot(hn, wgate_ref[...], preferred_element_type=F32) + bgate_ref[...]
    gate_ref[...] = _sigmoid(zg)


def _inproj(x2, norm_mix, w_rw, w_dsa, w_gate, b_gate, wuk_bd, kv_norm, tm):
    T, D = x2.shape
    grid = (T // tm,)
    row = lambda n: pl.BlockSpec((tm, n), lambda i: (i, 0))
    outs = [
        jax.ShapeDtypeStruct((T, RW_ZCOLS), F32),
        jax.ShapeDtypeStruct((T, DSA_HEADS * DSA_KV_RANK), BF16),
        jax.ShapeDtypeStruct((T, 2 * DSA_KV_RANK), BF16),
        jax.ShapeDtypeStruct((T, IDX_HEADS * LANE), BF16),
        jax.ShapeDtypeStruct((T, LANE), BF16),
        jax.ShapeDtypeStruct((T, LANE), F32),
        jax.ShapeDtypeStruct((T, 2 * D), F32),
    ]
    return pl.pallas_call(
        _inproj_kernel,
        grid=grid,
        in_specs=[row(D), _const_spec((1, D)), _const_spec(w_rw.shape), _const_spec(w_dsa.shape),
                  _const_spec(w_gate.shape), _const_spec((1, 2 * D)), _const_spec(wuk_bd.shape),
                  _const_spec((1, DSA_KV_RANK))],
        out_specs=[row(s.shape[1]) for s in outs],
        out_shape=outs,
        compiler_params=_cparams(("parallel",)),
        name="inproj",
    )(x2, norm_mix.reshape(1, D), w_rw, w_dsa, w_gate, b_gate.reshape(1, 2 * D), wuk_bd,
      kv_norm.reshape(1, DSA_KV_RANK))


def _rwkv_kernel(z_ref, mu_ref, w0_ref, w2_ref, a0_ref, a2_ref, g2_ref, kk_ref, ka_ref, rk_ref,
                 lnw_ref, lnb_ref, hs_ref, y_ref, state_ref, carry_ref, ybuf_ref):
    C, N, H = RW_CHUNK, RW_HEAD_DIM, RW_HEADS
    W = RW_WIDTH

    @pl.when(pl.program_id(1) == 0)
    def _():
        state_ref[...] = jnp.zeros_like(state_ref)
        carry_ref[...] = jnp.zeros_like(carry_ref)

    z = z_ref[...]
    row = lax.broadcasted_iota(I32, z.shape, 0)
    zprev = jnp.where(row == 0, carry_ref[...], pltpu.roll(z, 1, 0))
    carry_ref[...] = z[C - 1:C, :]
    zm = z + (zprev - z) * mu_ref[...]
    r = zm[:, 0:W]
    k = zm[:, W:2 * W]
    v = zm[:, 2 * W:3 * W]
    xw = zm[:, 3 * W:3 * W + LANE]
    xa = zm[:, 3 * W + LANE:3 * W + 2 * LANE]
    xg = zm[:, 3 * W + 2 * LANE:3 * W + 3 * LANE]

    hs = hs_ref[...]
    headsum = lambda t: _mm_exact_rhs(t, hs)

    wlog = -_softplus(-(w0_ref[...] + _mm(jnp.tanh(xw), w2_ref[...]))) - 0.5
    dlog = -jnp.exp(wlog)
    a = _sigmoid(a0_ref[...] + _mm(xa, a2_ref[...]))
    g = _mm(_sigmoid(xg), g2_ref[...])
    kk = k * kk_ref[...]
    kk = kk / jnp.maximum(jnp.sqrt(headsum(kk * kk)), 1e-12)
    k_eff = k * (1.0 + (a - 1.0) * ka_ref[...])
    bonus = headsum(r * k_eff * rk_ref[...]) * v

    ti = lax.broadcasted_iota(I32, (C, C), 0)
    si = lax.broadcasted_iota(I32, (C, C), 1)
    ltri = (si <= ti).astype(BF16)
    cum = _mm_exact_lhs(ltri, dlog)
    pin = jnp.exp(cum)
    pex = jnp.exp(cum - dlog)
    pinv = jnp.exp(-cum)
    a_t = -kk * pex
    r_t = r * pin
    b_t = kk * a * pinv
    k_t = k_eff * pinv
    p_c = pin[C - 1:C, :]

    t2 = lax.broadcasted_iota(I32, (C, 2 * C), 0)
    s2 = lax.broadcasted_iota(I32, (C, 2 * C), 1) % C
    m_strict = s2 < t2
    m_incl = s2 <= t2
    eye = (si == ti).astype(F32)
    zeros_cn = jnp.zeros((C, N), F32)

    hr = range(H)
    sls = [slice(h * N, (h + 1) * N) for h in hr]
    ar = [jnp.concatenate([a_t[:, sl], r_t[:, sl]], axis=0).astype(BF16) for sl in sls]
    bk = [jnp.concatenate([b_t[:, sl], k_t[:, sl]], axis=0).astype(BF16) for sl in sls]
    vh = [v[:, sl] for sl in sls]
    s0 = [state_ref[h] for h in hr]
    q = [_mm_nt(ar[h], bk[h]) for h in hr]
    w_all = [_mm_nt(ar[h], s0[h]) for h in hr]
    top = [jnp.where(m_strict, q[h][:C], 0.0).astype(BF16) for h in hr]
    bot = [jnp.where(m_incl, q[h][C:], 0.0).astype(BF16) for h in hr]
    rhs = [w_all[h][:C] + _mm(top[h], jnp.concatenate([zeros_cn, vh[h]], axis=0)) for h in hr]
    pw = [top[h][:, :C] for h in hr]
    inv = [eye + pw[h].astype(F32) for h in hr]
    for _ in range(5):
        pw = [_mm(pw[h], pw[h]).astype(BF16) for h in hr]
        inv = [inv[h] + _mm(pw[h], inv[h]) for h in hr]
    u = [_mm(inv[h], rhs[h]) for h in hr]
    gsrc = [jnp.concatenate([u[h], vh[h]], axis=0).astype(BF16) for h in hr]
    for h in hr:
        ybuf_ref[:, sls[h]] = w_all[h][C:] + _mm(bot[h], gsrc[h])
    for h in hr:
        state_ref[h] = (s0[h] + _mm_tn(gsrc[h], bk[h])) * p_c[:, sls[h]]

    y = ybuf_ref[...]
    mean = headsum(y) * (1.0 / N)
    d = y - mean
    var = headsum(d * d) * (1.0 / N)
    yn = d * lax.rsqrt(var + RW_GN_EPS) * lnw_ref[...] + lnb_ref[...]
    y_ref[...] = ((yn + bonus) * g).astype(y_ref.dtype)


def _rwkv(z_rw, B, S, mu, w0, w2, a0, a2, g2, k_k, k_a, r_k, ln_w, ln_b):
    C, W = RW_CHUNK, RW_WIDTH
    nc = S // C
    hid = jnp.arange(W) // RW_HEAD_DIM
    hs = (hid[:, None] == hid[None, :]).astype(BF16)
    vec = lambda t: t.reshape(1, -1)
    consts = [vec(mu), vec(w0), w2, vec(a0), a2, g2, vec(k_k), vec(k_a), vec(r_k), vec(ln_w),
              vec(ln_b), hs]
    return pl.pallas_call(
        _rwkv_kernel,
        grid=(B, nc),
        in_specs=[pl.BlockSpec((C, RW_ZCOLS), lambda b, c: (b * nc + c, 0))]
                 + [_const_spec(t.shape) for t in consts],
        out_specs=pl.BlockSpec((C, W), lambda b, c: (b * nc + c, 0)),
        out_shape=jax.ShapeDtypeStruct((B * S, W), BF16),
        scratch_shapes=[pltpu.VMEM((RW_HEADS, RW_HEAD_DIM, RW_HEAD_DIM), F32),
                        pltpu.VMEM((1, RW_ZCOLS), F32),
                        pltpu.VMEM((C, W), F32)],
        compiler_params=_cparams(("parallel", "arbitrary")),
        name="rwkv",
    )(z_rw, *consts)


def _dsa_kernel(qlat_ref, qidx_ref, widx_ref, ckv_ref, kidx_ref, wuv_ref, y_ref,
                keys_ref, thr_ref, need_ref, eqc_ref, m_ref, acc_ref, *, k_sel):
    TQ = DSA_TQ
    RC = 64
    qi = pl.program_id(1)
    nkt = qi + 1
    rowpos = qi * TQ + lax.broadcasted_iota(I32, (TQ, TQ), 0)
    lane = lax.broadcasted_iota(I32, (TQ, TQ), 1)

    def score_body(kt, carry):
        kx = kidx_ref[pl.ds(pl.multiple_of(kt * TQ, TQ), TQ), :]
        sc = jnp.zeros((TQ, TQ), F32)
        for h in range(IDX_HEADS):
            rel = jnp.maximum(_mm_nt(qidx_ref[:, h * LANE:(h + 1) * LANE], kx), 0.0)
            sc = sc + widx_ref[:, h:h + 1] * rel
        sc = jnp.where(sc == 0.0, 0.0, sc)
        bits = pltpu.bitcast(sc, I32)
        key = jnp.where(bits >= 0, bits, bits ^ jnp.int32(0x7FFFFFFF))
        keys_ref[kt] = jnp.where(kt * TQ + lane <= rowpos, key, jnp.int32(INT_MIN))
        return carry

    lax.fori_loop(0, nkt, score_body, 0)

    def count(rc, pred):
        def body(kt, acc):
            hit = pred(keys_ref[kt, rc * RC:(rc + 1) * RC, :]).astype(I32)
            return acc + hit[:, :LANE] + hit[:, LANE:]
        acc = lax.fori_loop(0, nkt, body, jnp.zeros((RC, LANE), I32))
        return jnp.sum(acc, axis=-1, keepdims=True)

    for rc in range(TQ // RC):
        def bit_body(i, tu):
            cand_u = tu | lax.shift_left(jnp.int32(1), 31 - i)
            cand = cand_u ^ jnp.int32(INT_MIN)
            cnt = count(rc, lambda kk: kk >= cand)
            return jnp.where(cnt >= k_sel, cand_u, tu)
        tu = lax.fori_loop(0, 32, bit_body, jnp.zeros((RC, 1), I32))
        thr = tu ^ jnp.int32(INT_MIN)
        n_gt = count(rc, lambda kk: kk > thr)
        thr_ref[rc * RC:(rc + 1) * RC, :] = thr
        need_ref[rc * RC:(rc + 1) * RC, :] = (k_sel - n_gt).astype(F32)

    eqc_ref[...] = jnp.zeros_like(eqc_ref)
    m_ref[...] = jnp.full_like(m_ref, NEG)
    acc_ref[...] = jnp.zeros_like(acc_ref)
    ustrict = (lax.broadcasted_iota(I32, (TQ, TQ), 0) < lane).astype(BF16)
    R = DSA_KV_RANK

    def att_body(kt, carry):
        key = keys_ref[kt]
        thr = thr_ref[...]
        eq = key == thr
        eqf = eq.astype(F32)
        pre = jnp.dot(eqf.astype(BF16), ustrict, preferred_element_type=F32) + eqc_ref[...]
        sel = ((key > thr) | (eq & (pre < need_ref[...]))) & (kt * TQ + lane <= rowpos)
        bias = jnp.where(sel, 0.0, NEG)
        eqc_ref[...] += jnp.sum(eqf, axis=-1, keepdims=True)
        ct = ckv_ref[pl.ds(pl.multiple_of(kt * TQ, TQ), TQ), :]
        ck = ct[:, :R]
        for h in range(DSA_HEADS):
            s = _mm_nt(qlat_ref[:, h * R:(h + 1) * R], ck) + bias
            m_old = m_ref[h]
            s_max = jnp.max(jnp.maximum(s[:, :LANE], s[:, LANE:]), axis=-1, keepdims=True)
            m_new = jnp.maximum(m_old, s_max)
            p = jnp.exp(s - jnp.concatenate([m_new, m_new], axis=-1))
            alpha = jnp.exp(m_old - m_new)
            acc_ref[h] = (jnp.concatenate([alpha, alpha], axis=-1) * acc_ref[h]
                          + jnp.dot(p.astype(BF16), ct, preferred_element_type=F32))
            m_ref[h] = m_new
        return carry

    lax.fori_loop(0, nkt, att_body, 0)
    o = jnp.concatenate([(acc_ref[h, :, :R] / acc_ref[h, :, R:]).astype(BF16)
                         for h in range(DSA_HEADS)], axis=-1)
    y_ref[...] = jnp.dot(o, wuv_ref[...], preferred_element_type=F32).astype(y_ref.dtype)


def _dsa(qlat, qidx, widx, ckv, kidx, wuv_bd, B, S):
    TQ = DSA_TQ
    nq = S // TQ
    k_sel = min(IDX_TOPK_MAX, S // 4)
    qrow = lambda n: pl.BlockSpec((TQ, n), lambda b, q: (b * nq + q, 0))
    brow = lambda n: pl.BlockSpec((S, n), lambda b, q: (b, 0))
    return pl.pallas_call(
        functools.partial(_dsa_kernel, k_sel=k_sel),
        grid=(B, nq),
        in_specs=[qrow(DSA_HEADS * DSA_KV_RANK), qrow(IDX_HEADS * LANE), qrow(LANE),
                  brow(2 * DSA_KV_RANK), brow(LANE), _const_spec(wuv_bd.shape)],
        out_specs=qrow(DSA_WIDTH),
        out_shape=jax.ShapeDtypeStruct((B * S, DSA_WIDTH), BF16),
        scratch_shapes=[pltpu.VMEM((nq, TQ, TQ), I32),
                        pltpu.VMEM((TQ, 1), I32), pltpu.VMEM((TQ, 1), F32), pltpu.VMEM((TQ, 1), F32),
                        pltpu.VMEM((DSA_HEADS, TQ, LANE), F32),
                        pltpu.VMEM((DSA_HEADS, TQ, 2 * DSA_KV_RANK), F32)],
        compiler_params=_cparams(("parallel", "arbitrary")),
        name="dsa",
    )(qlat, qidx, widx, ckv, kidx, wuv_bd)


def _memkv_kernel(m_ref, g_ref, w_ref, o_ref):
    o_ref[...] = _mm(_rms(m_ref[...], g_ref[...]), w_ref[...]).astype(o_ref.dtype)


def _memkv(mem2, norm_mem, w_ckv, M):
    R, D = mem2.shape
    return pl.pallas_call(
        _memkv_kernel,
        grid=(R // M,),
        in_specs=[pl.BlockSpec((M, D), lambda i: (i, 0)), _const_spec((1, D)), _const_spec(w_ckv.shape)],
        out_specs=pl.BlockSpec((M, w_ckv.shape[1]), lambda i: (i, 0)),
        out_shape=jax.ShapeDtypeStruct((R, w_ckv.shape[1]), BF16),
        compiler_params=_cparams(("parallel",)),
        name="memkv",
    )(mem2, norm_mem.reshape(1, D), w_ckv)


def _mix_kernel(x_ref, ya_ref, yb_ref, gate_ref, kv_ref, wpa_ref, wpb_ref, wout_ref, ncross_ref,
                wcq_ref, wco_ref, nffn_ref, wrh_ref, wrl_ref, br_ref,
                h2_ref, hn3_ref, ids_ref, gts_ref):
    D = x_ref.shape[1]
    xw = D // X_HEADS
    a = jnp.dot(ya_ref[...], wpa_ref[...], preferred_element_type=F32)
    b = jnp.dot(yb_ref[...], wpb_ref[...], preferred_element_type=F32)
    mix = gate_ref[:, :D] * a + gate_ref[:, D:] * b
    h1 = x_ref[...] + _mm(mix, wout_ref[...])
    q = _mm(_rms(h1, ncross_ref[...]), wcq_ref[...]).astype(BF16)
    outs = []
    for h in range(X_HEADS):
        s = _mm_nt(q[:, h * xw:(h + 1) * xw], kv_ref[:, h * xw:(h + 1) * xw]) * (xw ** -0.5)
        e = jnp.exp(s - jnp.max(s, axis=-1, keepdims=True))
        p = e / jnp.sum(e, axis=-1, keepdims=True)
        outs.append(_mm(p, kv_ref[:, D + h * xw:D + (h + 1) * xw]).astype(BF16))
    h2 = h1 + jnp.dot(jnp.concatenate(outs, axis=-1), wco_ref[...], preferred_element_type=F32)
    h2_ref[...] = h2
    hn3 = _rms(h2, nffn_ref[...])
    hn3_ref[:, 0, :] = hn3

    hi = hn3.astype(BF16)
    lo = (hn3 - hi.astype(F32)).astype(BF16)
    lg = (jnp.dot(hi, wrh_ref[...], preferred_element_type=F32)
          + jnp.dot(lo, wrh_ref[...], preferred_element_type=F32)
          + jnp.dot(hi, wrl_ref[...], preferred_element_type=F32)) + br_ref[...]
    lane = lax.broadcasted_iota(I32, lg.shape, 1)
    big = jnp.int32(1 << 20)
    rmax = lambda t: jnp.max(t, axis=-1, keepdims=True)
    rsum = lambda t: jnp.sum(t, axis=-1, keepdims=True)
    first = lambda hit: jnp.min(jnp.where(hit, lane, big), axis=-1, keepdims=True)
    gmask = lane < N_GROUPS
    gl = jnp.where(gmask, lg, NEG)
    ge = jnp.where(gmask, jnp.exp(gl - rmax(gl)), 0.0)
    gp = jnp.where(gmask, ge / rsum(ge), -1.0)
    p_grp = rmax(gp)
    grp = first(gp == p_grp)
    eidx = lane - N_GROUPS
    emask = (eidx >= 0) & (eidx < N_EXPERTS) & ((eidx // EXPERTS_PER_GROUP) == grp)
    el = jnp.where(emask, lg, NEG)
    ee = jnp.where(emask, jnp.exp(el - rmax(el)), 0.0)
    ep = jnp.where(emask, ee / rsum(ee), -1.0)
    p1 = rmax(ep)
    i1 = first(ep == p1)
    ep2 = jnp.where(lane == i1, -1.0, ep)
    p2 = rmax(ep2)
    i2 = first(ep2 == p2)
    den = p1 + p2
    ids_ref[...] = jnp.where(lane == 0, i1 - N_GROUPS, jnp.where(lane == 1, i2 - N_GROUPS, 0))
    gts_ref[...] = jnp.where(lane == 0, p_grp * p1 / den, jnp.where(lane == 1, p_grp * p2 / den, 0.0))


def _mix(x2, ya, yb, gates, kv, wpa, wpb, wout, ncross, wcq, wco, nffn, wr_hi, wr_lo, br, B, S, M, tm):
    T, D = x2.shape
    nt = S // tm
    row = lambda n: pl.BlockSpec((tm, n), lambda i: (i, 0))
    consts = [wpa, wpb, wout, ncross.reshape(1, D), wcq, wco, nffn.reshape(1, D), wr_hi, wr_lo, br]
    outs = [jax.ShapeDtypeStruct((T, D), F32), jax.ShapeDtypeStruct((T, 1, D), F32),
            jax.ShapeDtypeStruct((T, LANE), I32), jax.ShapeDtypeStruct((T, LANE), F32)]
    return pl.pallas_call(
        _mix_kernel,
        grid=(T // tm,),
        in_specs=[row(D), row(RW_WIDTH), row(DSA_WIDTH), row(2 * D),
                  pl.BlockSpec((M, 2 * D), lambda i: (i // nt, 0))]
                 + [_const_spec(t.shape) for t in consts],
        out_specs=[row(D), pl.BlockSpec((tm, 1, D), lambda i: (i, 0, 0)), row(LANE), row(LANE)],
        out_shape=outs,
        compiler_params=_cparams(("parallel",)),
        name="mix_cross_router",
    )(x2, ya, yb, gates, kv, *consts)


def _dispatch_kernel(slot_ref, src_ref, _, out_ref, sem):
    n_tok = src_ref.shape[0]

    def copy(a):
        return pltpu.make_async_copy(src_ref.at[a // 2], out_ref.at[slot_ref[0, 0, a]], sem)

    def start(a, c):
        copy(a).start()
        return c

    def wait(a, c):
        copy(a).wait()
        return c

    lax.fori_loop(0, 2 * n_tok, start, 0)
    lax.fori_loop(0, 2 * n_tok, wait, 0)


def _dispatch(slots, src, out_rows):
    T, _, D = src.shape
    td = ROW_TOK
    anyspec = pl.BlockSpec(memory_space=pl.ANY)
    return pl.pallas_call(
        _dispatch_kernel,
        grid=(T // td,),
        in_specs=[pl.BlockSpec((1, 1, 2 * td), lambda i: (i, 0, 0), memory_space=pltpu.SMEM),
                  pl.BlockSpec((td, 1, D), lambda i: (i, 0, 0)), anyspec],
        out_specs=anyspec,
        out_shape=jax.ShapeDtypeStruct((out_rows, 1, D), src.dtype),
        scratch_shapes=[pltpu.SemaphoreType.DMA(())],
        input_output_aliases={2: 0},
        compiler_params=_cparams(("arbitrary",)),
        name="moe_dispatch",
    )(slots.reshape(T // td, 1, 2 * td), src, jnp.zeros((out_rows, 1, D), src.dtype))


def _expert_kernel(be_ref, nu_ref, x_ref, wg_ref, wu_ref, wd_ref, o_ref):
    i = pl.program_id(0)

    @pl.when(i < nu_ref[0])
    def _():
        x = x_ref[:, 0, :].astype(BF16)
        gt = jnp.dot(x, wg_ref[...], preferred_element_type=F32)
        up = jnp.dot(x, wu_ref[...], preferred_element_type=F32)
        hid = gt * _sigmoid(gt) * up
        o_ref[:, 0, :] = _mm(hid, wd_ref[...])

    @pl.when(i >= nu_ref[0])
    def _():
        o_ref[...] = jnp.zeros_like(o_ref)


def _experts(xs, blk_expert, n_used, w_gate, w_up, w_down):
    P, _, D = xs.shape
    E, _, F = w_gate.shape
    nb = P // MOE_BLOCK
    grid_spec = pltpu.PrefetchScalarGridSpec(
        num_scalar_prefetch=2,
        grid=(nb,),
        in_specs=[pl.BlockSpec((MOE_BLOCK, 1, D), lambda i, be, nu: (i, 0, 0)),
                  pl.BlockSpec((None, D, F), lambda i, be, nu: (be[i], 0, 0)),
                  pl.BlockSpec((None, D, F), lambda i, be, nu: (be[i], 0, 0)),
                  pl.BlockSpec((None, F, D), lambda i, be, nu: (be[i], 0, 0))],
        out_specs=pl.BlockSpec((MOE_BLOCK, 1, D), lambda i, be, nu: (i, 0, 0)),
    )
    return pl.pallas_call(
        _expert_kernel,
        grid_spec=grid_spec,
        out_shape=jax.ShapeDtypeStruct((P, 1, D), F32),
        compiler_params=_cparams(("arbitrary",)),
        name="experts",
    )(blk_expert, n_used, xs, w_gate, w_up, w_down)


def _final_kernel(slot_ref, h_ref, gts_ref, g_ref, ys_ref, o_ref, buf_ref, sem, *, apply_norm):
    n_tok = h_ref.shape[0]

    def copy(a):
        return pltpu.make_async_copy(ys_ref.at[slot_ref[0, 0, a]], buf_ref.at[a % 2, a // 2], sem)

    def start(a, c):
        copy(a).start()
        return c

    def wait(a, c):
        copy(a).wait()
        return c

    lax.fori_loop(0, 2 * n_tok, start, 0)
    lax.fori_loop(0, 2 * n_tok, wait, 0)
    y = gts_ref[:, 0:1] * buf_ref[0, :, 0, :] + gts_ref[:, 1:2] * buf_ref[1, :, 0, :]
    h = h_ref[...] + y
    o_ref[...] = _rms(h, g_ref[...]) if apply_norm else h


def _final(h2, ys, slots, gts, norm_final, apply_norm):
    T, D = h2.shape
    tm = ROW_TOK
    row = lambda n: pl.BlockSpec((tm, n), lambda i: (i, 0))
    return pl.pallas_call(
        functools.partial(_final_kernel, apply_norm=apply_norm),
        grid=(T // tm,),
        in_specs=[pl.BlockSpec((1, 1, 2 * tm), lambda i: (i, 0, 0), memory_space=pltpu.SMEM),
                  row(D), row(LANE), _const_spec((1, D)), pl.BlockSpec(memory_space=pl.ANY)],
        out_specs=row(D),
        out_shape=jax.ShapeDtypeStruct((T, D), F32),
        scratch_shapes=[pltpu.VMEM((2, tm, 1, D), F32), pltpu.SemaphoreType.DMA(())],
        compiler_params=_cparams(("arbitrary",)),
        name="final",
    )(slots.reshape(T // tm, 1, 2 * tm), h2, gts, norm_final.reshape(1, D), ys)


def _pad_cols(w, n):
    return jnp.pad(w, ((0, 0), (0, n - w.shape[1])))


def _pad_rows(w, n):
    return jnp.pad(w, ((0, n - w.shape[0]), (0, 0)))


def _layer(h, mem, norm_mix, w_in, shift_mu, rw_w0, rw_w2, rw_a0, rw_a2, rw_g2, rw_k_k, rw_k_a,
           rw_r_k, rw_ln_w, rw_ln_b, kv_norm, w_uk, w_uv, w_proj_a, w_proj_b, b_gate, w_out,
           norm_cross, norm_mem, w_cq, w_ckv, w_co, norm_ffn, w_router_g, b_router_g, w_router_e,
           b_router_e, w_e_gate, w_e_up, w_e_down):
    B, S, D = h.shape
    M = mem.shape[1]
    T = B * S
    W = RW_WIDTH
    x2 = h.reshape(T, D)

    c = 0
    pieces = {}
    for name, n in (("r", W), ("k", W), ("v", W), ("xw", RW_DECAY_LORA), ("xa", RW_AAA_LORA),
                    ("xg", RW_GATE_LORA), ("q", DSA_WIDTH), ("ckv", DSA_KV_RANK),
                    ("qidx", IDX_HEADS * IDX_DIM), ("kidx", IDX_DIM), ("widx", IDX_HEADS),
                    ("gate", 2 * D)):
        pieces[name] = (c, n)
        c += n
    col = lambda name: w_in[:, pieces[name][0]:pieces[name][0] + pieces[name][1]]
    mu_of = lambda name: shift_mu[pieces[name][0]:pieces[name][0] + pieces[name][1]]
    w_rw = jnp.concatenate([col("r"), col("k"), col("v"), _pad_cols(col("xw"), LANE),
                            _pad_cols(col("xa"), LANE), _pad_cols(col("xg"), LANE)], axis=1).astype(BF16)
    padv = lambda t: jnp.pad(t, (0, LANE - t.shape[0]))
    mu = jnp.concatenate([mu_of("r"), mu_of("k"), mu_of("v"), padv(mu_of("xw")), padv(mu_of("xa")),
                          padv(mu_of("xg"))])
    qidx_w = col("qidx").reshape(D, IDX_HEADS, IDX_DIM)
    qidx_w = jnp.pad(qidx_w, ((0, 0), (0, 0), (0, LANE - IDX_DIM))).reshape(D, IDX_HEADS * LANE)
    w_dsa = jnp.concatenate([col("q"), col("ckv"), qidx_w, _pad_cols(col("kidx"), LANE),
                             _pad_cols(col("widx"), LANE)], axis=1).astype(BF16)
    w_gate_in = col("gate").astype(BF16)
    eye_h = jnp.eye(DSA_HEADS, dtype=F32)
    wuk_bd = jnp.einsum("rhd,hg->hdgr", w_uk, eye_h).reshape(DSA_WIDTH, DSA_HEADS * DSA_KV_RANK).astype(BF16)
    wuv_bd = jnp.einsum("rhd,hg->hrgd", w_uv, eye_h).reshape(DSA_HEADS * DSA_KV_RANK, DSA_WIDTH).astype(BF16)

    tm1 = min(256, T)
    z_rw, qlat, ckv, qidx, kidx, widx, gates = _inproj(
        x2, norm_mix, w_rw, w_dsa, w_gate_in, b_gate, wuk_bd, kv_norm, tm1)

    y_a = _rwkv(z_rw, B, S, mu, rw_w0, _pad_rows(rw_w2, LANE).astype(BF16), rw_a0,
                _pad_rows(rw_a2, LANE).astype(BF16), rw_g2.astype(BF16), rw_k_k, rw_k_a, rw_r_k,
                rw_ln_w, rw_ln_b)
    y_b = _dsa(qlat, qidx, widx, ckv, kidx, wuv_bd, B, S)
    kv = _memkv(mem.reshape(B * M, D), norm_mem, w_ckv.astype(BF16), M)

    w_r = jnp.concatenate([w_router_g, w_router_e], axis=1)
    w_r = _pad_cols(w_r, LANE)
    wr_hi = w_r.astype(BF16)
    wr_lo = (w_r - wr_hi.astype(F32)).astype(BF16)
    b_r = jnp.pad(jnp.concatenate([b_router_g, b_router_e]), (0, LANE - N_GROUPS - N_EXPERTS)).reshape(1, LANE)
    tm5 = min(256, S)
    h2, hn3, ids, gts = _mix(x2, y_a, y_b, gates, kv, w_proj_a.astype(BF16), w_proj_b.astype(BF16),
                             w_out.astype(BF16), norm_cross, w_cq.astype(BF16), w_co.astype(BF16),
                             norm_ffn, wr_hi, wr_lo, b_r, B, S, M, tm5)

    A = 2 * T
    e_flat = ids[:, :2].reshape(A)
    onehot = (e_flat[:, None] == jnp.arange(N_EXPERTS, dtype=I32)[None, :]).astype(I32)
    csum = jnp.cumsum(onehot, axis=0)
    rank = jnp.sum(csum * onehot, axis=1) - 1
    counts = csum[-1]
    padded = (counts + MOE_BLOCK - 1) // MOE_BLOCK * MOE_BLOCK
    pend = jnp.cumsum(padded)
    pstart = pend - padded
    slots = (jnp.sum(pstart[None, :] * onehot, axis=1) + rank).astype(I32)
    P = A + N_EXPERTS * MOE_BLOCK
    nb = P // MOE_BLOCK
    blk_pos = jnp.arange(nb, dtype=I32) * MOE_BLOCK
    blk_expert = jnp.minimum(jnp.sum((pend[None, :] <= blk_pos[:, None]).astype(I32), axis=1),
                             N_EXPERTS - 1).astype(I32)
    n_used = (pend[-1:] // MOE_BLOCK).astype(I32)

    xs = _dispatch(slots, hn3, P)
    ys = _experts(xs, blk_expert, n_used, w_e_gate.astype(BF16), w_e_up.astype(BF16),
                  w_e_down.astype(BF16))
    return h2, ys, slots, gts


def kernel(x, mem, norm_mix, w_in, shift_mu, rw_w0, rw_w2, rw_a0, rw_a2, rw_g2, rw_k_k, rw_k_a, rw_r_k, rw_ln_w, rw_ln_b, kv_norm, w_uk, w_uv, w_proj_a, w_proj_b, b_gate, w_out, norm_cross, norm_mem, w_cq, w_ckv, w_co, norm_ffn, w_router_g, b_router_g, w_router_e, b_router_e, w_e_gate, w_e_up, w_e_down, norm_final):
    B, S, D = x.shape
    depth = norm_mix.shape[0]
    h = x
    for l in range(depth):
        last = l == depth - 1
        h2, ys, slots, gts = _layer(
            h, mem, norm_mix[l], w_in[l], shift_mu[l], rw_w0[l], rw_w2[l], rw_a0[l], rw_a2[l],
            rw_g2[l], rw_k_k[l], rw_k_a[l], rw_r_k[l], rw_ln_w[l], rw_ln_b[l], kv_norm[l], w_uk[l],
            w_uv[l], w_proj_a[l], w_proj_b[l], b_gate[l], w_out[l], norm_cross[l], norm_mem[l],
            w_cq[l], w_ckv[l], w_co[l], norm_ffn[l], w_router_g[l], b_router_g[l], w_router_e[l],
            b_router_e[l], w_e_gate[l], w_e_up[l], w_e_down[l])
        h = _final(h2, ys, slots, gts, norm_final, last).reshape(B, S, D)
    return h
```

```python
import functools

import jax
import jax.numpy as jnp
from jax import lax
from jax.experimental import pallas as pl
from jax.experimental.pallas import tpu as pltpu

F32 = jnp.float32
BF16 = jnp.bfloat16
I32 = jnp.int32

RMS_EPS = 1e-6
RW_HEADS = 8
RW_HEAD_DIM = 64
RW_WIDTH = RW_HEADS * RW_HEAD_DIM
RW_DECAY_LORA = 64
RW_AAA_LORA = 64
RW_GATE_LORA = 128
RW_GN_EPS = 64e-5
DSA_HEADS = 8
DSA_HEAD_DIM = 64
DSA_WIDTH = DSA_HEADS * DSA_HEAD_DIM
DSA_KV_RANK = 128
IDX_HEADS = 4
IDX_DIM = 64
IDX_TOPK_MAX = 256
X_HEADS = 4
N_GROUPS = 4
EXPERTS_PER_GROUP = 8
N_EXPERTS = N_GROUPS * EXPERTS_PER_GROUP
MOE_BLOCK = 256

LANE = 128
RW_CHUNK = 64
RW_ZCOLS = 3 * RW_WIDTH + 3 * LANE
DSA_ZCOLS = DSA_WIDTH + DSA_KV_RANK + IDX_HEADS * LANE + LANE + LANE
DSA_TQ = 256
ROW_TOK = 256
NEG = -1e30
LOG2E = 1.4426950408889634
INT_MIN = -(2 ** 31)
VMEM_LIMIT = 56 * 1024 * 1024


def _mm(a, b):
    return jnp.dot(a.astype(BF16), b.astype(BF16), preferred_element_type=F32)


def _mm_nt(a, b):
    return lax.dot_general(a.astype(BF16), b.astype(BF16), (((1,), (1,)), ((), ())),
                           preferred_element_type=F32)


def _mm_tn(a, b):
    return lax.dot_general(a.astype(BF16), b.astype(BF16), (((0,), (0,)), ((), ())),
                           preferred_element_type=F32)


def _split3(x):
    hi = x.astype(BF16)
    r1 = x - hi.astype(F32)
    mid = r1.astype(BF16)
    lo = (r1 - mid.astype(F32)).astype(BF16)
    return hi, mid, lo


def _mm_exact_rhs(x, b):
    hi, mid, lo = _split3(x)
    return (jnp.dot(hi, b, preferred_element_type=F32) + jnp.dot(mid, b, preferred_element_type=F32)
            + jnp.dot(lo, b, preferred_element_type=F32))


def _mm_exact_lhs(a, x):
    hi, mid, lo = _split3(x)
    return (jnp.dot(a, hi, preferred_element_type=F32) + jnp.dot(a, mid, preferred_element_type=F32)
            + jnp.dot(a, lo, preferred_element_type=F32))


def _rms(x, g):
    return x * lax.rsqrt(jnp.mean(x * x, axis=-1, keepdims=True) + RMS_EPS) * g


def _sigmoid(x):
    return 1.0 / (1.0 + jnp.exp(-x))


def _softplus(x):
    return jnp.maximum(x, 0.0) + jnp.log1p(jnp.exp(-jnp.abs(x)))


def _cparams(sem):
    return pltpu.CompilerParams(dimension_semantics=sem, vmem_limit_bytes=VMEM_LIMIT)


def _const_spec(shape):
    nd = len(shape)
    return pl.BlockSpec(shape, lambda *_: (0,) * nd)


def _inproj_kernel(x_ref, g_ref, wrw_ref, wdsa_ref, wgate_ref, bgate_ref, wuk_ref, kvn_ref,
                   zrw_ref, qlat_ref, ckv_ref, qidx_ref, kidx_ref, widx_ref, gate_ref):
    hn = _rms(x_ref[...], g_ref[...]).astype(BF16)
    zrw_ref[...] = jnp.dot(hn, wrw_ref[...], preferred_element_type=F32)
    zd = jnp.dot(hn, wdsa_ref[...], preferred_element_type=F32)
    o = 0
    q = zd[:, o:o + DSA_WIDTH]; o += DSA_WIDTH
    c = zd[:, o:o + DSA_KV_RANK]; o += DSA_KV_RANK
    qi = zd[:, o:o + IDX_HEADS * LANE]; o += IDX_HEADS * LANE
    ki = zd[:, o:o + LANE]; o += LANE
    wi = zd[:, o:o + LANE]
    qlat_ref[...] = (_mm(q, wuk_ref[...]) * (DSA_HEAD_DIM ** -0.5 * LOG2E)).astype(BF16)
    cn = _rms(c, kvn_ref[...]).astype(BF16)
    ckv_ref[...] = jnp.concatenate([cn, jnp.ones_like(cn)], axis=-1)
    qidx_ref[...] = qi.astype(BF16)
    kidx_ref[...] = ki.astype(BF16)
    widx_ref[...] = wi * ((IDX_HEADS * IDX_DIM) ** -0.5)
    zg = jnp.dot(hn, wgate_ref[...], preferred_element_type=F32) + bgate_ref[...]
    gate_ref[...] = _sigmoid(zg)


def _inproj(x2, norm_mix, w_rw, w_dsa, w_gate, b_gate, wuk_bd, kv_norm, tm):
    T, D = x2.shape
    grid = (T // tm,)
    row = lambda n: pl.BlockSpec((tm, n), lambda i: (i, 0))
    outs = [
        jax.ShapeDtypeStruct((T, RW_ZCOLS), F32),
        jax.ShapeDtypeStruct((T, DSA_HEADS * DSA_KV_RANK), BF16),
        jax.ShapeDtypeStruct((T, 2 * DSA_KV_RANK), BF16),
        jax.ShapeDtypeStruct((T, IDX_HEADS * LANE), BF16),
        jax.ShapeDtypeStruct((T, LANE), BF16),
        jax.ShapeDtypeStruct((T, LANE), F32),
        jax.ShapeDtypeStruct((T, 2 * D), F32),
    ]
    return pl.pallas_call(
        _inproj_kernel,
        grid=grid,
        in_specs=[row(D), _const_spec((1, D)), _const_spec(w_rw.shape), _const_spec(w_dsa.shape),
                  _const_spec(w_gate.shape), _const_spec((1, 2 * D)), _const_spec(wuk_bd.shape),
                  _const_spec((1, DSA_KV_RANK))],
        out_specs=[row(s.shape[1]) for s in outs],
        out_shape=outs,
        compiler_params=_cparams(("parallel",)),
        name="inproj",
    )(x2, norm_mix.reshape(1, D), w_rw, w_dsa, w_gate, b_gate.reshape(1, 2 * D), wuk_bd,
      kv_norm.reshape(1, DSA_KV_RANK))


def _rwkv_kernel(z_ref, mu_ref, w0_ref, w2_ref, a0_ref, a2_ref, g2_ref, kk_ref, ka_ref, rk_ref,
                 lnw_ref, lnb_ref, hs_ref, y_ref, state_ref, carry_ref, ybuf_ref):
    C, N, H = RW_CHUNK, RW_HEAD_DIM, RW_HEADS
    W = RW_WIDTH

    @pl.when(pl.program_id(1) == 0)
    def _():
        state_ref[...] = jnp.zeros_like(state_ref)
        carry_ref[...] = jnp.zeros_like(carry_ref)

    z = z_ref[...]
    row = lax.broadcasted_iota(I32, z.shape, 0)
    zprev = jnp.where(row == 0, carry_ref[...], pltpu.roll(z, 1, 0))
    carry_ref[...] = z[C - 1:C, :]
    zm = z + (zprev - z) * mu_ref[...]
    r = zm[:, 0:W]
    k = zm[:, W:2 * W]
    v = zm[:, 2 * W:3 * W]
    xw = zm[:, 3 * W:3 * W + LANE]
    xa = zm[:, 3 * W + LANE:3 * W + 2 * LANE]
    xg = zm[:, 3 * W + 2 * LANE:3 * W + 3 * LANE]

    hs = hs_ref[...]
    headsum = lambda t: _mm_exact_rhs(t, hs)

    wlog = -_softplus(-(w0_ref[...] + _mm(jnp.tanh(xw), w2_ref[...]))) - 0.5
    dlog = -jnp.exp(wlog)
    a = _sigmoid(a0_ref[...] + _mm(xa, a2_ref[...]))
    g = _mm(_sigmoid(xg), g2_ref[...])
    kk = k * kk_ref[...]
    kk = kk / jnp.maximum(jnp.sqrt(headsum(kk * kk)), 1e-12)
    k_eff = k * (1.0 + (a - 1.0) * ka_ref[...])
    bonus = headsum(r * k_eff * rk_ref[...]) * v

    ti = lax.broadcasted_iota(I32, (C, C), 0)
    si = lax.broadcasted_iota(I32, (C, C), 1)
    ltri = (si <= ti).astype(BF16)
    cum = _mm_exact_lhs(ltri, dlog)
    pin = jnp.exp(cum)
    pex = jnp.exp(cum - dlog)
    pinv = jnp.exp(-cum)
    a_t = -kk * pex
    r_t = r * pin
    b_t = kk * a * pinv
    k_t = k_eff * pinv
    p_c = pin[C - 1:C, :]

    t2 = lax.broadcasted_iota(I32, (C, 2 * C), 0)
    s2 = lax.broadcasted_iota(I32, (C, 2 * C), 1) % C
    m_strict = s2 < t2
    m_incl = s2 <= t2
    eye = (si == ti).astype(F32)
    zeros_cn = jnp.zeros((C, N), F32)

    hr = range(H)
    sls = [slice(h * N, (h + 1) * N) for h in hr]
    ar = [jnp.concatenate([a_t[:, sl], r_t[:, sl]], axis=0).astype(BF16) for sl in sls]
    bk = [jnp.concatenate([b_t[:, sl], k_t[:, sl]], axis=0).astype(BF16) for sl in sls]
    vh = [v[:, sl] for sl in sls]
    s0 = [state_ref[h] for h in hr]
    q = [_mm_nt(ar[h], bk[h]) for h in hr]
    w_all = [_mm_nt(ar[h], s0[h]) for h in hr]
    top = [jnp.where(m_strict, q[h][:C], 0.0).astype(BF16) for h in hr]
    bot = [jnp.where(m_incl, q[h][C:], 0.0).astype(BF16) for h in hr]
    rhs = [w_all[h][:C] + _mm(top[h], jnp.concatenate([zeros_cn, vh[h]], axis=0)) for h in hr]
    pw = [top[h][:, :C] for h in hr]
    inv = [eye + pw[h].astype(F32) for h in hr]
    for _ in range(5):
        pw = [_mm(pw[h], pw[h]).astype(BF16) for h in hr]
        inv = [inv[h] + _mm(pw[h], inv[h]) for h in hr]
    u = [_mm(inv[h], rhs[h]) for h in hr]
    gsrc = [jnp.concatenate([u[h], vh[h]], axis=0).astype(BF16) for h in hr]
    for h in hr:
        ybuf_ref[:, sls[h]] = w_all[h][C:] + _mm(bot[h], gsrc[h])
    for h in hr:
        state_ref[h] = (s0[h] + _mm_tn(gsrc[h], bk[h])) * p_c[:, sls[h]]

    y = ybuf_ref[...]
    mean = headsum(y) * (1.0 / N)
    d = y - mean
    var = headsum(d * d) * (1.0 / N)
    yn = d * lax.rsqrt(var + RW_GN_EPS) * lnw_ref[...] + lnb_ref[...]
    y_ref[...] = ((yn + bonus) * g).astype(y_ref.dtype)


def _rwkv(z_rw, B, S, mu, w0, w2, a0, a2, g2, k_k, k_a, r_k, ln_w, ln_b):
    C, W = RW_CHUNK, RW_WIDTH
    nc = S // C
    hid = jnp.arange(W) // RW_HEAD_DIM
    hs = (hid[:, None] == hid[None, :]).astype(BF16)
    vec = lambda t: t.reshape(1, -1)
    consts = [vec(mu), vec(w0), w2, vec(a0), a2, g2, vec(k_k), vec(k_a), vec(r_k), vec(ln_w),
              vec(ln_b), hs]
    return pl.pallas_call(
        _rwkv_kernel,
        grid=(B, nc),
        in_specs=[pl.BlockSpec((C, RW_ZCOLS), lambda b, c: (b * nc + c, 0))]
                 + [_const_spec(t.shape) for t in consts],
        out_specs=pl.BlockSpec((C, W), lambda b, c: (b * nc + c, 0)),
        out_shape=jax.ShapeDtypeStruct((B * S, W), BF16),
        scratch_shapes=[pltpu.VMEM((RW_HEADS, RW_HEAD_DIM, RW_HEAD_DIM), F32),
                        pltpu.VMEM((1, RW_ZCOLS), F32),
                        pltpu.VMEM((C, W), F32)],
        compiler_params=_cparams(("parallel", "arbitrary")),
        name="rwkv",
    )(z_rw, *consts)


def _dsa_kernel(qlat_ref, qidx_ref, widx_ref, ckv_ref, kidx_ref, wuv_ref, y_ref,
                keys_ref, thr_ref, need_ref, eqc_ref, bias_ref, m_ref, acc_ref, *, k_sel):
    TQ = DSA_TQ
    RC = 64
    qi = pl.program_id(1)
    nkt = qi + 1
    rowpos = qi * TQ + lax.broadcasted_iota(I32, (TQ, TQ), 0)
    lane = lax.broadcasted_iota(I32, (TQ, TQ), 1)

    def score_body(kt, carry):
        kx = kidx_ref[pl.ds(pl.multiple_of(kt * TQ, TQ), TQ), :]
        sc = jnp.zeros((TQ, TQ), F32)
        for h in range(IDX_HEADS):
            rel = jnp.maximum(_mm_nt(qidx_ref[:, h * LANE:(h + 1) * LANE], kx), 0.0)
            sc = sc + widx_ref[:, h:h + 1] * rel
        sc = jnp.where(sc == 0.0, 0.0, sc)
        bits = pltpu.bitcast(sc, I32)
        key = jnp.where(bits >= 0, bits, bits ^ jnp.int32(0x7FFFFFFF))
        keys_ref[kt] = jnp.where(kt * TQ + lane <= rowpos, key, jnp.int32(INT_MIN))
        return carry

    lax.fori_loop(0, nkt, score_body, 0)

    keys_ref[nkt] = jnp.full((TQ, TQ), INT_MIN, I32)
    npair = (nkt + 1) // 2
    ones_ll = jnp.ones((LANE, LANE), BF16)
    rep2 = lambda t: jnp.concatenate([t, t], axis=-1)

    def count(level, strict):
        accs = []
        for rc in range(TQ // RC):
            lv = level[rc * RC:(rc + 1) * RC]

            def body(kp, acc, rc=rc, lv=lv):
                for j in range(2):
                    for half in range(TQ // LANE):
                        kk = keys_ref[2 * kp + j, rc * RC:(rc + 1) * RC, half * LANE:(half + 1) * LANE]
                        acc = acc + ((kk > lv) if strict else (kk >= lv)).astype(I32)
                return acc

            accs.append(lax.fori_loop(0, npair, body, jnp.zeros((RC, LANE), I32)))
        acc = jnp.concatenate(accs, axis=0).astype(F32).astype(BF16)
        return jnp.dot(acc, ones_ll, preferred_element_type=F32)

    def bit_body(i, tu):
        cand_u = tu | lax.shift_left(jnp.int32(1), 31 - i)
        cnt = count(cand_u ^ jnp.int32(INT_MIN), strict=False)
        return jnp.where(cnt >= k_sel, cand_u, tu)

    tu = lax.fori_loop(0, 32, bit_body, jnp.zeros((TQ, LANE), I32))
    thr = tu ^ jnp.int32(INT_MIN)
    thr_ref[...] = thr
    need_ref[...] = k_sel - count(thr, strict=True)

    eqc_ref[...] = jnp.zeros_like(eqc_ref)
    m_ref[...] = jnp.full_like(m_ref, NEG)
    acc_ref[...] = jnp.zeros_like(acc_ref)
    ucol = lax.broadcasted_iota(I32, (TQ, TQ + LANE), 1)
    uext = ((lax.broadcasted_iota(I32, (TQ, TQ + LANE), 0) < ucol) | (ucol >= TQ)).astype(BF16)
    R = DSA_KV_RANK

    def att_body(kt, carry):
        key = keys_ref[kt]
        thr = rep2(thr_ref[...])
        eq = key == thr
        pre = jnp.dot(eq.astype(F32).astype(BF16), uext, preferred_element_type=F32)
        eqc = eqc_ref[...]
        sel = ((key > thr) | (eq & (pre[:, :TQ] + rep2(eqc) < rep2(need_ref[...])))) & (kt * TQ + lane <= rowpos)
        bias = jnp.where(sel, 0.0, NEG)
        eqc_ref[...] = eqc + pre[:, TQ:]
        ct = ckv_ref[pl.ds(pl.multiple_of(kt * TQ, TQ), TQ), :]
        ck = ct[:, :R]
        bias_ref[...] = bias
        HQ = TQ // 2
        for h in range(DSA_HEADS):
            for r0 in (0, HQ):
                rows = pl.ds(r0, HQ)
                s = _mm_nt(qlat_ref[rows, h * R:(h + 1) * R], ck) + bias_ref[rows, :]
                m_old = m_ref[h, rows, :]
                s_max = jnp.max(jnp.maximum(s[:, :LANE], s[:, LANE:]), axis=-1, keepdims=True)
                m_new = jnp.maximum(m_old, s_max)
                p = jnp.exp2(s - rep2(m_new))
                acc_ref[h, rows, :] = (rep2(jnp.exp2(m_old - m_new)) * acc_ref[h, rows, :]
                                       + jnp.dot(p.astype(BF16), ct, preferred_element_type=F32))
                m_ref[h, rows, :] = m_new
        return carry

    lax.fori_loop(0, nkt, att_body, 0)
    o = jnp.concatenate([(acc_ref[h, :, :R] / acc_ref[h, :, R:]).astype(BF16)
                         for h in range(DSA_HEADS)], axis=-1)
    y_ref[...] = jnp.dot(o, wuv_ref[...], preferred_element_type=F32).astype(y_ref.dtype)


def _dsa(qlat, qidx, widx, ckv, kidx, wuv_bd, B, S):
    TQ = DSA_TQ
    nq = S // TQ
    k_sel = min(IDX_TOPK_MAX, S // 4)
    qrow = lambda n: pl.BlockSpec((TQ, n), lambda b, q: (b * nq + q, 0))
    brow = lambda n: pl.BlockSpec((S, n), lambda b, q: (b, 0))
    return pl.pallas_call(
        functools.partial(_dsa_kernel, k_sel=k_sel),
        grid=(B, nq),
        in_specs=[qrow(DSA_HEADS * DSA_KV_RANK), qrow(IDX_HEADS * LANE), qrow(LANE),
                  brow(2 * DSA_KV_RANK), brow(LANE), _const_spec(wuv_bd.shape)],
        out_specs=qrow(DSA_WIDTH),
        out_shape=jax.ShapeDtypeStruct((B * S, DSA_WIDTH), BF16),
        scratch_shapes=[pltpu.VMEM((nq + 1, TQ, TQ), I32),
                        pltpu.VMEM((TQ, LANE), I32), pltpu.VMEM((TQ, LANE), F32), pltpu.VMEM((TQ, LANE), F32),
                        pltpu.VMEM((TQ, TQ), F32),
                        pltpu.VMEM((DSA_HEADS, TQ, LANE), F32),
                        pltpu.VMEM((DSA_HEADS, TQ, 2 * DSA_KV_RANK), F32)],
        compiler_params=_cparams(("parallel", "arbitrary")),
        name="dsa",
    )(qlat, qidx, widx, ckv, kidx, wuv_bd)


def _memkv_kernel(m_ref, g_ref, w_ref, o_ref):
    o_ref[...] = _mm(_rms(m_ref[...], g_ref[...]), w_ref[...]).astype(o_ref.dtype)


def _memkv(mem2, norm_mem, w_ckv, M):
    R, D = mem2.shape
    return pl.pallas_call(
        _memkv_kernel,
        grid=(R // M,),
        in_specs=[pl.BlockSpec((M, D), lambda i: (i, 0)), _const_spec((1, D)), _const_spec(w_ckv.shape)],
        out_specs=pl.BlockSpec((M, w_ckv.shape[1]), lambda i: (i, 0)),
        out_shape=jax.ShapeDtypeStruct((R, w_ckv.shape[1]), BF16),
        compiler_params=_cparams(("parallel",)),
        name="memkv",
    )(mem2, norm_mem.reshape(1, D), w_ckv)


def _mix_kernel(x_ref, ya_ref, yb_ref, gate_ref, kv_ref, wpa_ref, wpb_ref, wout_ref, ncross_ref,
                wcq_ref, wco_ref, nffn_ref, wrh_ref, wrl_ref, br_ref,
                h2_ref, hn3_ref, ids_ref, gts_ref):
    D = x_ref.shape[1]
    xw = D // X_HEADS
    a = jnp.dot(ya_ref[...], wpa_ref[...], preferred_element_type=F32)
    b = jnp.dot(yb_ref[...], wpb_ref[...], preferred_element_type=F32)
    mix = gate_ref[:, :D] * a + gate_ref[:, D:] * b
    h1 = x_ref[...] + _mm(mix, wout_ref[...])
    q = _mm(_rms(h1, ncross_ref[...]), wcq_ref[...]).astype(BF16)
    outs = []
    for h in range(X_HEADS):
        s = _mm_nt(q[:, h * xw:(h + 1) * xw], kv_ref[:, h * xw:(h + 1) * xw]) * (xw ** -0.5)
        e = jnp.exp(s - jnp.max(s, axis=-1, keepdims=True))
        p = e / jnp.sum(e, axis=-1, keepdims=True)
        outs.append(_mm(p, kv_ref[:, D + h * xw:D + (h + 1) * xw]).astype(BF16))
    h2 = h1 + jnp.dot(jnp.concatenate(outs, axis=-1), wco_ref[...], preferred_element_type=F32)
    h2_ref[...] = h2
    hn3 = _rms(h2, nffn_ref[...])
    hn3_ref[:, 0, :] = hn3

    hi = hn3.astype(BF16)
    lo = (hn3 - hi.astype(F32)).astype(BF16)
    lg = (jnp.dot(hi, wrh_ref[...], preferred_element_type=F32)
          + jnp.dot(lo, wrh_ref[...], preferred_element_type=F32)
          + jnp.dot(hi, wrl_ref[...], preferred_element_type=F32)) + br_ref[...]
    lane = lax.broadcasted_iota(I32, lg.shape, 1)
    big = jnp.int32(1 << 20)
    rmax = lambda t: jnp.max(t, axis=-1, keepdims=True)
    rsum = lambda t: jnp.sum(t, axis=-1, keepdims=True)
    first = lambda hit: jnp.min(jnp.where(hit, lane, big), axis=-1, keepdims=True)
    gmask = lane < N_GROUPS
    gl = jnp.where(gmask, lg, NEG)
    ge = jnp.where(gmask, jnp.exp(gl - rmax(gl)), 0.0)
    gp = jnp.where(gmask, ge / rsum(ge), -1.0)
    p_grp = rmax(gp)
    grp = first(gp == p_grp)
    eidx = lane - N_GROUPS
    emask = (eidx >= 0) & (eidx < N_EXPERTS) & ((eidx // EXPERTS_PER_GROUP) == grp)
    el = jnp.where(emask, lg, NEG)
    ee = jnp.where(emask, jnp.exp(el - rmax(el)), 0.0)
    ep = jnp.where(emask, ee / rsum(ee), -1.0)
    p1 = rmax(ep)
    i1 = first(ep == p1)
    ep2 = jnp.where(lane == i1, -1.0, ep)
    p2 = rmax(ep2)
    i2 = first(ep2 == p2)
    den = p1 + p2
    ids_ref[...] = jnp.where(lane == 0, i1 - N_GROUPS, jnp.where(lane == 1, i2 - N_GROUPS, 0))
    gts_ref[...] = jnp.where(lane == 0, p_grp * p1 / den, jnp.where(lane == 1, p_grp * p2 / den, 0.0))


def _mix(x2, ya, yb, gates, kv, wpa, wpb, wout, ncross, wcq, wco, nffn, wr_hi, wr_lo, br, B, S, M, tm):
    T, D = x2.shape
    nt = S // tm
    row = lambda n: pl.BlockSpec((tm, n), lambda i: (i, 0))
    consts = [wpa, wpb, wout, ncross.reshape(1, D), wcq, wco, nffn.reshape(1, D), wr_hi, wr_lo, br]
    outs = [jax.ShapeDtypeStruct((T, D), F32), jax.ShapeDtypeStruct((T, 1, D), F32),
            jax.ShapeDtypeStruct((T, LANE), I32), jax.ShapeDtypeStruct((T, LANE), F32)]
    return pl.pallas_call(
        _mix_kernel,
        grid=(T // tm,),
        in_specs=[row(D), row(RW_WIDTH), row(DSA_WIDTH), row(2 * D),
                  pl.BlockSpec((M, 2 * D), lambda i: (i // nt, 0))]
                 + [_const_spec(t.shape) for t in consts],
        out_specs=[row(D), pl.BlockSpec((tm, 1, D), lambda i: (i, 0, 0)), row(LANE), row(LANE)],
        out_shape=outs,
        compiler_params=_cparams(("parallel",)),
        name="mix_cross_router",
    )(x2, ya, yb, gates, kv, *consts)


def _dispatch_kernel(slot_ref, src_ref, _, out_ref, sem):
    n_tok = src_ref.shape[0]

    def start(t, c):
        for j in range(2):
            pltpu.make_async_copy(src_ref.at[t], out_ref.at[slot_ref[0, 0, 2 * t + j]], sem).start()
        return c

    lax.fori_loop(0, n_tok, start, 0)
    for j in range(2):
        pltpu.make_async_copy(src_ref, out_ref.at[pl.ds(0, n_tok)], sem).wait()


def _dispatch(slots, src, out_rows):
    T, SUB, _ = src.shape
    td = ROW_TOK
    anyspec = pl.BlockSpec(memory_space=pl.ANY)
    return pl.pallas_call(
        _dispatch_kernel,
        grid=(T // td,),
        in_specs=[pl.BlockSpec((1, 1, 2 * td), lambda i: (i, 0, 0), memory_space=pltpu.SMEM),
                  pl.BlockSpec((td, SUB, LANE), lambda i: (i, 0, 0)), anyspec],
        out_specs=anyspec,
        out_shape=jax.ShapeDtypeStruct((out_rows, SUB, LANE), src.dtype),
        scratch_shapes=[pltpu.SemaphoreType.DMA(())],
        input_output_aliases={2: 0},
        compiler_params=_cparams(("arbitrary",)),
        name="moe_dispatch",
    )(slots.reshape(T // td, 1, 2 * td), src, jnp.zeros((out_rows, SUB, LANE), src.dtype))


def _expert_kernel(be_ref, nu_ref, x_ref, wg_ref, wu_ref, wd_ref, o_ref):
    i = pl.program_id(0)

    @pl.when(i < nu_ref[0])
    def _():
        x = x_ref[:, 0, :].astype(BF16)
        gt = jnp.dot(x, wg_ref[...], preferred_element_type=F32)
        up = jnp.dot(x, wu_ref[...], preferred_element_type=F32)
        hid = gt * _sigmoid(gt) * up
        o_ref[:, 0, :] = _mm(hid, wd_ref[...])

    @pl.when(i >= nu_ref[0])
    def _():
        o_ref[...] = jnp.zeros_like(o_ref)


def _experts(xs, blk_expert, n_used, w_gate, w_up, w_down):
    P, _, D = xs.shape
    E, _, F = w_gate.shape
    nb = P // MOE_BLOCK
    grid_spec = pltpu.PrefetchScalarGridSpec(
        num_scalar_prefetch=2,
        grid=(nb,),
        in_specs=[pl.BlockSpec((MOE_BLOCK, 1, D), lambda i, be, nu: (i, 0, 0)),
                  pl.BlockSpec((None, D, F), lambda i, be, nu: (be[i], 0, 0)),
                  pl.BlockSpec((None, D, F), lambda i, be, nu: (be[i], 0, 0)),
                  pl.BlockSpec((None, F, D), lambda i, be, nu: (be[i], 0, 0))],
        out_specs=pl.BlockSpec((MOE_BLOCK, 1, D), lambda i, be, nu: (i, 0, 0)),
    )
    return pl.pallas_call(
        _expert_kernel,
        grid_spec=grid_spec,
        out_shape=jax.ShapeDtypeStruct((P, 1, D), F32),
        compiler_params=_cparams(("arbitrary",)),
        name="experts",
    )(blk_expert, n_used, xs, w_gate, w_up, w_down)


def _final_kernel(slot_ref, h_ref, gts_ref, g_ref, ys_ref, o_ref, buf_ref, sem, *, apply_norm):
    n_tok = h_ref.shape[0]

    def start(t, c):
        for j in range(2):
            pltpu.make_async_copy(ys_ref.at[slot_ref[0, 0, 2 * t + j]], buf_ref.at[j, t], sem).start()
        return c

    lax.fori_loop(0, n_tok, start, 0)
    for j in range(2):
        pltpu.make_async_copy(ys_ref.at[pl.ds(0, n_tok)], buf_ref.at[j], sem).wait()
    y = gts_ref[:, 0:1] * buf_ref[0, :, 0, :] + gts_ref[:, 1:2] * buf_ref[1, :, 0, :]
    h = h_ref[...] + y
    o_ref[...] = _rms(h, g_ref[...]) if apply_norm else h


def _final(h2, ys, slots, gts, norm_final, apply_norm):
    T, D = h2.shape
    tm = ROW_TOK
    row = lambda n: pl.BlockSpec((tm, n), lambda i: (i, 0))
    return pl.pallas_call(
        functools.partial(_final_kernel, apply_norm=apply_norm),
        grid=(T // tm,),
        in_specs=[pl.BlockSpec((1, 1, 2 * tm), lambda i: (i, 0, 0), memory_space=pltpu.SMEM),
                  row(D), row(LANE), _const_spec((1, D)), pl.BlockSpec(memory_space=pl.ANY)],
        out_specs=row(D),
        out_shape=jax.ShapeDtypeStruct((T, D), F32),
        scratch_shapes=[pltpu.VMEM((2, tm, 1, D), F32), pltpu.SemaphoreType.DMA(())],
        compiler_params=_cparams(("arbitrary",)),
        name="final",
    )(slots.reshape(T // tm, 1, 2 * tm), h2, gts, norm_final.reshape(1, D), ys)


def _pad_cols(w, n):
    return jnp.pad(w, ((0, 0), (0, n - w.shape[1])))


def _pad_rows(w, n):
    return jnp.pad(w, ((0, n - w.shape[0]), (0, 0)))


def _layer(h, mem, norm_mix, w_in, shift_mu, rw_w0, rw_w2, rw_a0, rw_a2, rw_g2, rw_k_k, rw_k_a,
           rw_r_k, rw_ln_w, rw_ln_b, kv_norm, w_uk, w_uv, w_proj_a, w_proj_b, b_gate, w_out,
           norm_cross, norm_mem, w_cq, w_ckv, w_co, norm_ffn, w_router_g, b_router_g, w_router_e,
           b_router_e, w_e_gate, w_e_up, w_e_down):
    B, S, D = h.shape
    M = mem.shape[1]
    T = B * S
    W = RW_WIDTH
    x2 = h.reshape(T, D)

    c = 0
    pieces = {}
    for name, n in (("r", W), ("k", W), ("v", W), ("xw", RW_DECAY_LORA), ("xa", RW_AAA_LORA),
                    ("xg", RW_GATE_LORA), ("q", DSA_WIDTH), ("ckv", DSA_KV_RANK),
                    ("qidx", IDX_HEADS * IDX_DIM), ("kidx", IDX_DIM), ("widx", IDX_HEADS),
                    ("gate", 2 * D)):
        pieces[name] = (c, n)
        c += n
    col = lambda name: w_in[:, pieces[name][0]:pieces[name][0] + pieces[name][1]]
    mu_of = lambda name: shift_mu[pieces[name][0]:pieces[name][0] + pieces[name][1]]
    w_rw = jnp.concatenate([col("r"), col("k"), col("v"), _pad_cols(col("xw"), LANE),
                            _pad_cols(col("xa"), LANE), _pad_cols(col("xg"), LANE)], axis=1).astype(BF16)
    padv = lambda t: jnp.pad(t, (0, LANE - t.shape[0]))
    mu = jnp.concatenate([mu_of("r"), mu_of("k"), mu_of("v"), padv(mu_of("xw")), padv(mu_of("xa")),
                          padv(mu_of("xg"))])
    qidx_w = col("qidx").reshape(D, IDX_HEADS, IDX_DIM)
    qidx_w = jnp.pad(qidx_w, ((0, 0), (0, 0), (0, LANE - IDX_DIM))).reshape(D, IDX_HEADS * LANE)
    w_dsa = jnp.concatenate([col("q"), col("ckv"), qidx_w, _pad_cols(col("kidx"), LANE),
                             _pad_cols(col("widx"), LANE)], axis=1).astype(BF16)
    w_gate_in = col("gate").astype(BF16)
    eye_h = jnp.eye(DSA_HEADS, dtype=F32)
    wuk_bd = jnp.einsum("rhd,hg->hdgr", w_uk, eye_h).reshape(DSA_WIDTH, DSA_HEADS * DSA_KV_RANK).astype(BF16)
    wuv_bd = jnp.einsum("rhd,hg->hrgd", w_uv, eye_h).reshape(DSA_HEADS * DSA_KV_RANK, DSA_WIDTH).astype(BF16)

    tm1 = min(256, T)
    z_rw, qlat, ckv, qidx, kidx, widx, gates = _inproj(
        x2, norm_mix, w_rw, w_dsa, w_gate_in, b_gate, wuk_bd, kv_norm, tm1)

    y_a = _rwkv(z_rw, B, S, mu, rw_w0, _pad_rows(rw_w2, LANE).astype(BF16), rw_a0,
                _pad_rows(rw_a2, LANE).astype(BF16), rw_g2.astype(BF16), rw_k_k, rw_k_a, rw_r_k,
                rw_ln_w, rw_ln_b)
    y_b = _dsa(qlat, qidx, widx, ckv, kidx, wuv_bd, B, S)
    kv = _memkv(mem.reshape(B * M, D), norm_mem, w_ckv.astype(BF16), M)

    w_r = jnp.concatenate([w_router_g, w_router_e], axis=1)
    w_r = _pad_cols(w_r, LANE)
    wr_hi = w_r.astype(BF16)
    wr_lo = (w_r - wr_hi.astype(F32)).astype(BF16)
    b_r = jnp.pad(jnp.concatenate([b_router_g, b_router_e]), (0, LANE - N_GROUPS - N_EXPERTS)).reshape(1, LANE)
    tm5 = min(256, S)
    h2, hn3, ids, gts = _mix(x2, y_a, y_b, gates, kv, w_proj_a.astype(BF16), w_proj_b.astype(BF16),
                             w_out.astype(BF16), norm_cross, w_cq.astype(BF16), w_co.astype(BF16),
                             norm_ffn, wr_hi, wr_lo, b_r, B, S, M, tm5)

    A = 2 * T
    e_flat = ids[:, :2].reshape(A)
    onehot = (e_flat[:, None] == jnp.arange(N_EXPERTS, dtype=I32)[None, :]).astype(I32)
    csum = jnp.cumsum(onehot, axis=0)
    rank = jnp.sum(csum * onehot, axis=1) - 1
    counts = csum[-1]
    padded = (counts + MOE_BLOCK - 1) // MOE_BLOCK * MOE_BLOCK
    pend = jnp.cumsum(padded)
    pstart = pend - padded
    slots = (jnp.sum(pstart[None, :] * onehot, axis=1) + rank).astype(I32)
    P = A + N_EXPERTS * MOE_BLOCK
    nb = P // MOE_BLOCK
    blk_pos = jnp.arange(nb, dtype=I32) * MOE_BLOCK
    blk_expert = jnp.minimum(jnp.sum((pend[None, :] <= blk_pos[:, None]).astype(I32), axis=1),
                             N_EXPERTS - 1).astype(I32)
    n_used = (pend[-1:] // MOE_BLOCK).astype(I32)

    xs = _dispatch(slots, hn3.reshape(T, D // LANE, LANE), P).reshape(P, 1, D)
    ys = _experts(xs, blk_expert, n_used, w_e_gate.astype(BF16), w_e_up.astype(BF16),
                  w_e_down.astype(BF16))
    return h2, ys, slots, gts


def kernel(x, mem, norm_mix, w_in, shift_mu, rw_w0, rw_w2, rw_a0, rw_a2, rw_g2, rw_k_k, rw_k_a, rw_r_k, rw_ln_w, rw_ln_b, kv_norm, w_uk, w_uv, w_proj_a, w_proj_b, b_gate, w_out, norm_cross, norm_mem, w_cq, w_ckv, w_co, norm_ffn, w_router_g, b_router_g, w_router_e, b_router_e, w_e_gate, w_e_up, w_e_down, norm_final):
    B, S, D = x.shape
    depth = norm_mix.shape[0]
    h = x
    for l in range(depth):
        last = l == depth - 1
        h2, ys, slots, gts = _layer(
            h, mem, norm_mix[l], w_in[l], shift_mu[l], rw_w0[l], rw_w2[l], rw_a0[l], rw_a2[l],
            rw_g2[l], rw_k_k[l], rw_k_a[l], rw_r_k[l], rw_ln_w[l], rw_ln_b[l], kv_norm[l], w_uk[l],
            w_uv[l], w_proj_a[l], w_proj_b[l], b_gate[l], w_out[l], norm_cross[l], norm_mem[l],
            w_cq[l], w_ckv[l], w_co[l], norm_ffn[l], w_router_g[l], b_router_g[l], w_router_e[l],
            b_router_e[l], w_e_gate[l], w_e_up[l], w_e_down[l])
        h = _final(h2, ys, slots, gts, norm_final, last).reshape(B, S, D)
    return h
```

```python
import functools

import jax
import jax.numpy as jnp
from jax import lax
from jax.experimental import pallas as pl
from jax.experimental.pallas import tpu as pltpu

F32 = jnp.float32
BF16 = jnp.bfloat16
I32 = jnp.int32

RMS_EPS = 1e-6
RW_HEADS = 8
RW_HEAD_DIM = 64
RW_WIDTH = RW_HEADS * RW_HEAD_DIM
RW_DECAY_LORA = 64
RW_AAA_LORA = 64
RW_GATE_LORA = 128
RW_GN_EPS = 64e-5
DSA_HEADS = 8
DSA_HEAD_DIM = 64
DSA_WIDTH = DSA_HEADS * DSA_HEAD_DIM
DSA_KV_RANK = 128
IDX_HEADS = 4
IDX_DIM = 64
IDX_TOPK_MAX = 256
X_HEADS = 4
N_GROUPS = 4
EXPERTS_PER_GROUP = 8
N_EXPERTS = N_GROUPS * EXPERTS_PER_GROUP
MOE_BLOCK = 256

LANE = 128
RW_CHUNK = 64
RW_ZCOLS = 3 * RW_WIDTH + 3 * LANE
DSA_ZCOLS = DSA_WIDTH + DSA_KV_RANK + IDX_HEADS * LANE + LANE + LANE
DSA_TQ = 512
DSA_TK = 256
ROW_TOK = 256
NEG = -1e30
LOG2E = 1.4426950408889634
INT_MIN = -(2 ** 31)
VMEM_LIMIT = 56 * 1024 * 1024


def _mm(a, b):
    return jnp.dot(a.astype(BF16), b.astype(BF16), preferred_element_type=F32)


def _mm_nt(a, b):
    return lax.dot_general(a.astype(BF16), b.astype(BF16), (((1,), (1,)), ((), ())),
                           preferred_element_type=F32)


def _mm_tn(a, b):
    return lax.dot_general(a.astype(BF16), b.astype(BF16), (((0,), (0,)), ((), ())),
                           preferred_element_type=F32)


def _split3(x):
    hi = x.astype(BF16)
    r1 = x - hi.astype(F32)
    mid = r1.astype(BF16)
    lo = (r1 - mid.astype(F32)).astype(BF16)
    return hi, mid, lo


def _mm_hilo_rhs(x, b):
    hi = x.astype(BF16)
    lo = (x - hi.astype(F32)).astype(BF16)
    return jnp.dot(hi, b, preferred_element_type=F32) + jnp.dot(lo, b, preferred_element_type=F32)


def _mm_exact_lhs(a, x):
    hi, mid, lo = _split3(x)
    return (jnp.dot(a, hi, preferred_element_type=F32) + jnp.dot(a, mid, preferred_element_type=F32)
            + jnp.dot(a, lo, preferred_element_type=F32))


def _rms(x, g):
    return x * lax.rsqrt(jnp.mean(x * x, axis=-1, keepdims=True) + RMS_EPS) * g


def _sigmoid(x):
    return 1.0 / (1.0 + jnp.exp(-x))


def _softplus(x):
    return jnp.maximum(x, 0.0) + jnp.log1p(jnp.exp(-jnp.abs(x)))


def _load_row_tiles(ref, rows, sub):
    return jnp.concatenate([ref[pl.ds(c, rows, stride=sub), :] for c in range(sub)], axis=-1)


def _store_row_tiles(ref, val):
    rows, d = val.shape
    sub = d // LANE
    for c in range(sub):
        ref[pl.ds(c, rows, stride=sub), :] = val[:, c * LANE:(c + 1) * LANE]


def _cparams(sem):
    return pltpu.CompilerParams(dimension_semantics=sem, vmem_limit_bytes=VMEM_LIMIT)


def _const_spec(shape):
    nd = len(shape)
    return pl.BlockSpec(shape, lambda *_: (0,) * nd)


def _inproj_kernel(x_ref, g_ref, wrw_ref, wdsa_ref, wgate_ref, bgate_ref, wuk_ref, kvn_ref,
                   zrw_ref, qlat_ref, ckv_ref, qidx_ref, kidx_ref, widx_ref, gate_ref):
    hn = _rms(x_ref[...], g_ref[...]).astype(BF16)
    zrw_ref[...] = jnp.dot(hn, wrw_ref[...], preferred_element_type=F32)
    zd = jnp.dot(hn, wdsa_ref[...], preferred_element_type=F32)
    o = 0
    q = zd[:, o:o + DSA_WIDTH]; o += DSA_WIDTH
    c = zd[:, o:o + DSA_KV_RANK]; o += DSA_KV_RANK
    qi = zd[:, o:o + IDX_HEADS * LANE]; o += IDX_HEADS * LANE
    ki = zd[:, o:o + LANE]; o += LANE
    wi = zd[:, o:o + LANE]
    qlat_ref[...] = (_mm(q, wuk_ref[...]) * (DSA_HEAD_DIM ** -0.5 * LOG2E)).astype(BF16)
    cn = _rms(c, kvn_ref[...]).astype(BF16)
    ckv_ref[...] = jnp.concatenate([cn, jnp.ones_like(cn)], axis=-1)
    qidx_ref[...] = qi.astype(BF16)
    kidx_ref[...] = ki.astype(BF16)
    widx_ref[...] = wi * ((IDX_HEADS * IDX_DIM) ** -0.5)
    zg = jnp.dot(hn, wgate_ref[...], preferred_element_type=F32) + bgate_ref[...]
    gate_ref[...] = _sigmoid(zg)


def _inproj(x2, norm_mix, w_rw, w_dsa, w_gate, b_gate, wuk_bd, kv_norm, tm):
    T, D = x2.shape
    grid = (T // tm,)
    row = lambda n: pl.BlockSpec((tm, n), lambda i: (i, 0))
    outs = [
        jax.ShapeDtypeStruct((T, RW_ZCOLS), F32),
        jax.ShapeDtypeStruct((T, DSA_HEADS * DSA_KV_RANK), BF16),
        jax.ShapeDtypeStruct((T, 2 * DSA_KV_RANK), BF16),
        jax.ShapeDtypeStruct((T, IDX_HEADS * LANE), BF16),
        jax.ShapeDtypeStruct((T, LANE), BF16),
        jax.ShapeDtypeStruct((T, LANE), F32),
        jax.ShapeDtypeStruct((T, 2 * D), F32),
    ]
    return pl.pallas_call(
        _inproj_kernel,
        grid=grid,
        in_specs=[row(D), _const_spec((1, D)), _const_spec(w_rw.shape), _const_spec(w_dsa.shape),
                  _const_spec(w_gate.shape), _const_spec((1, 2 * D)), _const_spec(wuk_bd.shape),
                  _const_spec((1, DSA_KV_RANK))],
        out_specs=[row(s.shape[1]) for s in outs],
        out_shape=outs,
        compiler_params=_cparams(("parallel",)),
        name="inproj",
    )(x2, norm_mix.reshape(1, D), w_rw, w_dsa, w_gate, b_gate.reshape(1, 2 * D), wuk_bd,
      kv_norm.reshape(1, DSA_KV_RANK))


def _rwkv_kernel(z_ref, mu_ref, w0_ref, w2_ref, a0_ref, a2_ref, g2_ref, kk_ref, ka_ref, rk_ref,
                 lnw_ref, lnb_ref, hs_ref, y_ref, state_ref, carry_ref, ybuf_ref):
    C, N, H = RW_CHUNK, RW_HEAD_DIM, RW_HEADS
    W = RW_WIDTH

    @pl.when(pl.program_id(1) == 0)
    def _():
        state_ref[...] = jnp.zeros_like(state_ref)
        carry_ref[...] = jnp.zeros_like(carry_ref)

    z = z_ref[...]
    row = lax.broadcasted_iota(I32, z.shape, 0)
    zprev = jnp.where(row == 0, carry_ref[...], pltpu.roll(z, 1, 0))
    carry_ref[...] = z[C - 1:C, :]
    zm = z + (zprev - z) * mu_ref[...]
    r = zm[:, 0:W]
    k = zm[:, W:2 * W]
    v = zm[:, 2 * W:3 * W]
    xw = zm[:, 3 * W:3 * W + LANE]
    xa = zm[:, 3 * W + LANE:3 * W + 2 * LANE]
    xg = zm[:, 3 * W + 2 * LANE:3 * W + 3 * LANE]

    hs = hs_ref[...]
    headsum = lambda t: _mm_hilo_rhs(t, hs)

    wlog = -_softplus(-(w0_ref[...] + _mm(jnp.tanh(xw), w2_ref[...]))) - 0.5
    dlog = -jnp.exp(wlog)
    a = _sigmoid(a0_ref[...] + _mm(xa, a2_ref[...]))
    g = _mm(_sigmoid(xg), g2_ref[...])
    kk = k * kk_ref[...]
    kk = kk / jnp.maximum(jnp.sqrt(headsum(kk * kk)), 1e-12)
    k_eff = k * (1.0 + (a - 1.0) * ka_ref[...])
    bonus = headsum(r * k_eff * rk_ref[...]) * v

    ti = lax.broadcasted_iota(I32, (C, C), 0)
    si = lax.broadcasted_iota(I32, (C, C), 1)
    ltri = (si <= ti).astype(BF16)
    cum = _mm_exact_lhs(ltri, dlog)
    pin = jnp.exp(cum)
    pex = jnp.exp(cum - dlog)
    pinv = jnp.exp(-cum)
    a_t = -kk * pex
    r_t = r * pin
    b_t = kk * a * pinv
    k_t = k_eff * pinv
    p_c = pin[C - 1:C, :]

    t2 = lax.broadcasted_iota(I32, (C, 2 * C), 0)
    s2 = lax.broadcasted_iota(I32, (C, 2 * C), 1) % C
    m_strict = s2 < t2
    m_incl = s2 <= t2
    eye = (si == ti).astype(F32)
    zeros_cn = jnp.zeros((C, N), F32)

    hr = range(H)
    sls = [slice(h * N, (h + 1) * N) for h in hr]
    ar = [jnp.concatenate([a_t[:, sl], r_t[:, sl]], axis=0).astype(BF16) for sl in sls]
    bk = [jnp.concatenate([b_t[:, sl], k_t[:, sl]], axis=0).astype(BF16) for sl in sls]
    vh = [v[:, sl] for sl in sls]
    s0 = [state_ref[h] for h in hr]
    q = [_mm_nt(ar[h], bk[h]) for h in hr]
    w_all = [_mm_nt(ar[h], s0[h]) for h in hr]
    top = [jnp.where(m_strict, q[h][:C], 0.0).astype(BF16) for h in hr]
    bot = [jnp.where(m_incl, q[h][C:], 0.0).astype(BF16) for h in hr]
    rhs = [w_all[h][:C] + _mm(top[h], jnp.concatenate([zeros_cn, vh[h]], axis=0)) for h in hr]
    pw = [top[h][:, :C] for h in hr]
    inv = [eye + pw[h].astype(F32) for h in hr]
    for _ in range(5):
        pw = [_mm(pw[h], pw[h]).astype(BF16) for h in hr]
        inv = [inv[h] + _mm(pw[h], inv[h]) for h in hr]
    u = [_mm(inv[h], rhs[h]) for h in hr]
    gsrc = [jnp.concatenate([u[h], vh[h]], axis=0).astype(BF16) for h in hr]
    for h in hr:
        ybuf_ref[:, sls[h]] = w_all[h][C:] + _mm(bot[h], gsrc[h])
    for h in hr:
        state_ref[h] = (s0[h] + _mm_tn(gsrc[h], bk[h])) * p_c[:, sls[h]]

    y = ybuf_ref[...]
    mean = headsum(y) * (1.0 / N)
    d = y - mean
    var = headsum(d * d) * (1.0 / N)
    yn = d * lax.rsqrt(var + RW_GN_EPS) * lnw_ref[...] + lnb_ref[...]
    y_ref[...] = ((yn + bonus) * g).astype(y_ref.dtype)


def _rwkv(z_rw, B, S, mu, w0, w2, a0, a2, g2, k_k, k_a, r_k, ln_w, ln_b):
    C, W = RW_CHUNK, RW_WIDTH
    nc = S // C
    hid = jnp.arange(W) // RW_HEAD_DIM
    hs = (hid[:, None] == hid[None, :]).astype(BF16)
    vec = lambda t: t.reshape(1, -1)
    consts = [vec(mu), vec(w0), w2, vec(a0), a2, g2, vec(k_k), vec(k_a), vec(r_k), vec(ln_w),
              vec(ln_b), hs]
    return pl.pallas_call(
        _rwkv_kernel,
        grid=(B, nc),
        in_specs=[pl.BlockSpec((C, RW_ZCOLS), lambda b, c: (b * nc + c, 0))]
                 + [_const_spec(t.shape) for t in consts],
        out_specs=pl.BlockSpec((C, W), lambda b, c: (b * nc + c, 0)),
        out_shape=jax.ShapeDtypeStruct((B * S, W), BF16),
        scratch_shapes=[pltpu.VMEM((RW_HEADS, RW_HEAD_DIM, RW_HEAD_DIM), F32),
                        pltpu.VMEM((1, RW_ZCOLS), F32),
                        pltpu.VMEM((C, W), F32)],
        compiler_params=_cparams(("parallel", "arbitrary")),
        name="rwkv",
    )(z_rw, *consts)


def _dsa_kernel(qlat_ref, qidx_ref, widx_ref, ckv_ref, kidx_ref, wuv_ref, y_ref,
                keys_ref, hi_ref, lo_ref, thr_ref, need_ref, eqc_ref, bias_ref, m_ref, acc_ref, *, k_sel):
    TQ, TK = DSA_TQ, DSA_TK
    RC = 128
    HQ = 128
    I16 = jnp.int16
    HALF = 1 << 15
    qi = pl.program_id(1)
    nkt = (qi + 1) * (TQ // TK)
    npair = nkt // 2
    rowpos = qi * TQ + lax.broadcasted_iota(I32, (TQ, TK), 0)
    lane = lax.broadcasted_iota(I32, (TQ, TK), 1)
    ones_ll = jnp.ones((LANE, LANE), BF16)
    rep2 = lambda t: jnp.concatenate([t, t], axis=-1)

    def score_body(kt, carry):
        kx = kidx_ref[pl.ds(pl.multiple_of(kt * TK, TK), TK), :]
        sc = jnp.zeros((TQ, TK), F32)
        for h in range(IDX_HEADS):
            rel = jnp.maximum(_mm_nt(qidx_ref[:, h * LANE:(h + 1) * LANE], kx), 0.0)
            sc = sc + widx_ref[:, h:h + 1] * rel
        sc = jnp.where(sc == 0.0, 0.0, sc)
        bits = pltpu.bitcast(sc, I32)
        key = jnp.where(bits >= 0, bits, bits ^ jnp.int32(0x7FFFFFFF))
        key = jnp.where(kt * TK + lane <= rowpos, key, jnp.int32(INT_MIN))
        keys_ref[kt] = key
        hi_ref[kt] = (key >> 16).astype(I16)
        return carry

    lax.fori_loop(0, nkt, score_body, 0)

    def count16(ref, level, strict):
        accs = []
        for rc in range(TQ // RC):
            lv = level[rc * RC:(rc + 1) * RC].astype(I16)

            def body(kp, acc, rc=rc, lv=lv):
                for j in range(2):
                    for half in range(TK // LANE):
                        kk = ref[2 * kp + j, rc * RC:(rc + 1) * RC, half * LANE:(half + 1) * LANE]
                        acc = acc + ((kk > lv) if strict else (kk >= lv)).astype(I16)
                return acc

            accs.append(lax.fori_loop(0, npair, body, jnp.zeros((RC, LANE), I16)))
        acc = jnp.concatenate(accs, axis=0).astype(I32).astype(F32).astype(BF16)
        return jnp.dot(acc, ones_ll, preferred_element_type=F32)

    def select16(ref, k_row):
        def bit_body(i, tu):
            cand = tu | lax.shift_left(jnp.int32(1), 15 - i)
            return jnp.where(count16(ref, cand - HALF, strict=False) >= k_row, cand, tu)
        return lax.fori_loop(0, 16, bit_body, jnp.zeros((TQ, LANE), I32))

    th = select16(hi_ref, jnp.full((TQ, LANE), k_sel, F32)) - HALF
    n_gt_hi = count16(hi_ref, th, strict=True)
    thr_ref[...] = th

    def lo_body(kt, carry):
        key = keys_ref[kt]
        lo = (key & jnp.int32(0xFFFF)) - HALF
        lo_ref[kt] = jnp.where((key >> 16) == rep2(thr_ref[...]), lo, -HALF).astype(I16)
        return carry

    lax.fori_loop(0, nkt, lo_body, 0)
    tl = select16(lo_ref, k_sel - n_gt_hi)
    n_gt = n_gt_hi + count16(lo_ref, tl - HALF, strict=True)
    thr_ref[...] = th * (1 << 16) + tl
    need_ref[...] = k_sel - n_gt

    eqc_ref[...] = jnp.zeros_like(eqc_ref)
    m_ref[...] = jnp.full_like(m_ref, NEG)
    acc_ref[...] = jnp.zeros_like(acc_ref)
    ucol = lax.broadcasted_iota(I32, (TK, TK + LANE), 1)
    uext = ((lax.broadcasted_iota(I32, (TK, TK + LANE), 0) < ucol) | (ucol >= TK)).astype(BF16)
    R = DSA_KV_RANK

    def att_body(kt, carry):
        key = keys_ref[kt]
        thr = rep2(thr_ref[...])
        eq = key == thr
        pre = jnp.dot(eq.astype(F32).astype(BF16), uext, preferred_element_type=F32)
        eqc = eqc_ref[...]
        sel = ((key > thr) | (eq & (pre[:, :TK] + rep2(eqc) < rep2(need_ref[...])))) & (kt * TK + lane <= rowpos)
        bias_ref[...] = jnp.where(sel, 0.0, NEG)
        eqc_ref[...] = eqc + pre[:, TK:]
        ct = ckv_ref[pl.ds(pl.multiple_of(kt * TK, TK), TK), :]
        ck = ct[:, :R]
        for h in range(DSA_HEADS):
            for r0 in range(0, TQ, HQ):
                rows = pl.ds(r0, HQ)
                s = _mm_nt(qlat_ref[rows, h * R:(h + 1) * R], ck) + bias_ref[rows, :]
                m_old = m_ref[h, rows, :]
                s_max = jnp.max(jnp.maximum(s[:, :LANE], s[:, LANE:]), axis=-1, keepdims=True)
                m_new = jnp.maximum(m_old, s_max)
                p = jnp.exp2(s - rep2(m_new))
                acc_ref[h, rows, :] = (rep2(jnp.exp2(m_old - m_new)) * acc_ref[h, rows, :]
                                       + jnp.dot(p.astype(BF16), ct, preferred_element_type=F32))
                m_ref[h, rows, :] = m_new
        return carry

    lax.fori_loop(0, nkt, att_body, 0)
    o = jnp.concatenate([(acc_ref[h, :, :R] / acc_ref[h, :, R:]).astype(BF16)
                         for h in range(DSA_HEADS)], axis=-1)
    y_ref[...] = jnp.dot(o, wuv_ref[...], preferred_element_type=F32).astype(y_ref.dtype)


def _dsa(qlat, qidx, widx, ckv, kidx, wuv_bd, B, S):
    TQ, TK = DSA_TQ, DSA_TK
    assert TQ == 2 * TK and TK == 2 * LANE and S % TQ == 0
    nq = S // TQ
    nk = S // TK
    k_sel = min(IDX_TOPK_MAX, S // 4)
    qrow = lambda n: pl.BlockSpec((TQ, n), lambda b, q: (b * nq + q, 0))
    brow = lambda n: pl.BlockSpec((S, n), lambda b, q: (b, 0))
    return pl.pallas_call(
        functools.partial(_dsa_kernel, k_sel=k_sel),
        grid=(B, nq),
        in_specs=[qrow(DSA_HEADS * DSA_KV_RANK), qrow(IDX_HEADS * LANE), qrow(LANE),
                  brow(2 * DSA_KV_RANK), brow(LANE), _const_spec(wuv_bd.shape)],
        out_specs=qrow(DSA_WIDTH),
        out_shape=jax.ShapeDtypeStruct((B * S, DSA_WIDTH), BF16),
        scratch_shapes=[pltpu.VMEM((nk, TQ, TK), I32), pltpu.VMEM((nk, TQ, TK), jnp.int16),
                        pltpu.VMEM((nk, TQ, TK), jnp.int16),
                        pltpu.VMEM((TQ, LANE), I32), pltpu.VMEM((TQ, LANE), F32), pltpu.VMEM((TQ, LANE), F32),
                        pltpu.VMEM((TQ, TK), F32),
                        pltpu.VMEM((DSA_HEADS, TQ, LANE), F32),
                        pltpu.VMEM((DSA_HEADS, TQ, 2 * DSA_KV_RANK), F32)],
        compiler_params=_cparams(("parallel", "arbitrary")),
        name="dsa",
    )(qlat, qidx, widx, ckv, kidx, wuv_bd)


def _memkv_kernel(m_ref, g_ref, w_ref, o_ref):
    o_ref[...] = _mm(_rms(m_ref[...], g_ref[...]), w_ref[...]).astype(o_ref.dtype)


def _memkv(mem2, norm_mem, w_ckv, M):
    R, D = mem2.shape
    return pl.pallas_call(
        _memkv_kernel,
        grid=(R // M,),
        in_specs=[pl.BlockSpec((M, D), lambda i: (i, 0)), _const_spec((1, D)), _const_spec(w_ckv.shape)],
        out_specs=pl.BlockSpec((M, w_ckv.shape[1]), lambda i: (i, 0)),
        out_shape=jax.ShapeDtypeStruct((R, w_ckv.shape[1]), BF16),
        compiler_params=_cparams(("parallel",)),
        name="memkv",
    )(mem2, norm_mem.reshape(1, D), w_ckv)


def _mix_kernel(x_ref, ya_ref, yb_ref, gate_ref, kv_ref, wpa_ref, wpb_ref, wout_ref, ncross_ref,
                wcq_ref, wco_ref, nffn_ref, wrh_ref, wrl_ref, br_ref,
                h2_ref, hn3_ref, ids_ref, gts_ref):
    D = x_ref.shape[1]
    xw = D // X_HEADS
    a = jnp.dot(ya_ref[...], wpa_ref[...], preferred_element_type=F32)
    b = jnp.dot(yb_ref[...], wpb_ref[...], preferred_element_type=F32)
    mix = gate_ref[:, :D] * a + gate_ref[:, D:] * b
    h1 = x_ref[...] + _mm(mix, wout_ref[...])
    q = _mm(_rms(h1, ncross_ref[...]), wcq_ref[...]).astype(BF16)
    outs = []
    for h in range(X_HEADS):
        s = _mm_nt(q[:, h * xw:(h + 1) * xw], kv_ref[:, h * xw:(h + 1) * xw]) * (xw ** -0.5)
        e = jnp.exp(s - jnp.max(s, axis=-1, keepdims=True))
        p = e / jnp.sum(e, axis=-1, keepdims=True)
        outs.append(_mm(p, kv_ref[:, D + h * xw:D + (h + 1) * xw]).astype(BF16))
    h2 = h1 + jnp.dot(jnp.concatenate(outs, axis=-1), wco_ref[...], preferred_element_type=F32)
    h2_ref[...] = h2
    hn3 = _rms(h2, nffn_ref[...])
    _store_row_tiles(hn3_ref, hn3)

    hi = hn3.astype(BF16)
    lo = (hn3 - hi.astype(F32)).astype(BF16)
    lg = (jnp.dot(hi, wrh_ref[...], preferred_element_type=F32)
          + jnp.dot(lo, wrh_ref[...], preferred_element_type=F32)
          + jnp.dot(hi, wrl_ref[...], preferred_element_type=F32)) + br_ref[...]
    lane = lax.broadcasted_iota(I32, lg.shape, 1)
    big = jnp.int32(1 << 20)
    rmax = lambda t: jnp.max(t, axis=-1, keepdims=True)
    rsum = lambda t: jnp.sum(t, axis=-1, keepdims=True)
    first = lambda hit: jnp.min(jnp.where(hit, lane, big), axis=-1, keepdims=True)
    gmask = lane < N_GROUPS
    gl = jnp.where(gmask, lg, NEG)
    ge = jnp.where(gmask, jnp.exp(gl - rmax(gl)), 0.0)
    gp = jnp.where(gmask, ge / rsum(ge), -1.0)
    p_grp = rmax(gp)
    grp = first(gp == p_grp)
    eidx = lane - N_GROUPS
    emask = (eidx >= 0) & (eidx < N_EXPERTS) & ((eidx // EXPERTS_PER_GROUP) == grp)
    el = jnp.where(emask, lg, NEG)
    ee = jnp.where(emask, jnp.exp(el - rmax(el)), 0.0)
    ep = jnp.where(emask, ee / rsum(ee), -1.0)
    p1 = rmax(ep)
    i1 = first(ep == p1)
    ep2 = jnp.where(lane == i1, -1.0, ep)
    p2 = rmax(ep2)
    i2 = first(ep2 == p2)
    den = p1 + p2
    ids_ref[...] = jnp.where(lane == 0, i1 - N_GROUPS, jnp.where(lane == 1, i2 - N_GROUPS, 0))
    gts_ref[...] = jnp.where(lane == 0, p_grp * p1 / den, jnp.where(lane == 1, p_grp * p2 / den, 0.0))


def _mix(x2, ya, yb, gates, kv, wpa, wpb, wout, ncross, wcq, wco, nffn, wr_hi, wr_lo, br, B, S, M, tm):
    T, D = x2.shape
    nt = S // tm
    row = lambda n: pl.BlockSpec((tm, n), lambda i: (i, 0))
    consts = [wpa, wpb, wout, ncross.reshape(1, D), wcq, wco, nffn.reshape(1, D), wr_hi, wr_lo, br]
    outs = [jax.ShapeDtypeStruct((T, D), F32), jax.ShapeDtypeStruct((T * (D // LANE), LANE), F32),
            jax.ShapeDtypeStruct((T, LANE), I32), jax.ShapeDtypeStruct((T, LANE), F32)]
    return pl.pallas_call(
        _mix_kernel,
        grid=(T // tm,),
        in_specs=[row(D), row(RW_WIDTH), row(DSA_WIDTH), row(2 * D),
                  pl.BlockSpec((M, 2 * D), lambda i: (i // nt, 0))]
                 + [_const_spec(t.shape) for t in consts],
        out_specs=[row(D), pl.BlockSpec((tm * (D // LANE), LANE), lambda i: (i, 0)), row(LANE), row(LANE)],
        out_shape=outs,
        compiler_params=_cparams(("parallel",)),
        name="mix_cross_router",
    )(x2, ya, yb, gates, kv, *consts)


def _dispatch_kernel(slot_ref, src_ref, _, out_ref, sem, *, sub):
    n_tok = src_ref.shape[0] // sub
    tile = lambda ref, r: ref.at[pl.ds(pl.multiple_of(r * sub, sub), sub), :]

    def start(t, c):
        for j in range(2):
            pltpu.make_async_copy(tile(src_ref, t), tile(out_ref, slot_ref[0, 0, 2 * t + j]), sem).start()
        return c

    lax.fori_loop(0, n_tok, start, 0)
    for j in range(2):
        pltpu.make_async_copy(src_ref, out_ref.at[pl.ds(0, n_tok * sub), :], sem).wait()


def _dispatch(slots, src, out_rows, sub):
    T = src.shape[0] // sub
    td = ROW_TOK
    anyspec = pl.BlockSpec(memory_space=pl.ANY)
    return pl.pallas_call(
        functools.partial(_dispatch_kernel, sub=sub),
        grid=(T // td,),
        in_specs=[pl.BlockSpec((1, 1, 2 * td), lambda i: (i, 0, 0), memory_space=pltpu.SMEM),
                  pl.BlockSpec((td * sub, LANE), lambda i: (i, 0)), anyspec],
        out_specs=anyspec,
        out_shape=jax.ShapeDtypeStruct((out_rows * sub, LANE), src.dtype),
        scratch_shapes=[pltpu.SemaphoreType.DMA(())],
        input_output_aliases={2: 0},
        compiler_params=_cparams(("arbitrary",)),
        name="moe_dispatch",
    )(slots.reshape(T // td, 1, 2 * td), src, jnp.zeros((out_rows * sub, LANE), src.dtype))


def _expert_kernel(be_ref, nu_ref, x_ref, wg_ref, wu_ref, wd_ref, o_ref):
    i = pl.program_id(0)

    @pl.when(i < nu_ref[0])
    def _():
        sub = wg_ref.shape[0] // LANE
        x = _load_row_tiles(x_ref, MOE_BLOCK, sub).astype(BF16)
        gt = jnp.dot(x, wg_ref[...], preferred_element_type=F32)
        up = jnp.dot(x, wu_ref[...], preferred_element_type=F32)
        hid = gt * _sigmoid(gt) * up
        _store_row_tiles(o_ref, _mm(hid, wd_ref[...]))

    @pl.when(i >= nu_ref[0])
    def _():
        o_ref[...] = jnp.zeros_like(o_ref)


def _experts(xs, blk_expert, n_used, w_gate, w_up, w_down):
    E, D, F = w_gate.shape
    sub = D // LANE
    P = xs.shape[0] // sub
    nb = P // MOE_BLOCK
    grid_spec = pltpu.PrefetchScalarGridSpec(
        num_scalar_prefetch=2,
        grid=(nb,),
        in_specs=[pl.BlockSpec((MOE_BLOCK * sub, LANE), lambda i, be, nu: (i, 0)),
                  pl.BlockSpec((None, D, F), lambda i, be, nu: (be[i], 0, 0)),
                  pl.BlockSpec((None, D, F), lambda i, be, nu: (be[i], 0, 0)),
                  pl.BlockSpec((None, F, D), lambda i, be, nu: (be[i], 0, 0))],
        out_specs=pl.BlockSpec((MOE_BLOCK * sub, LANE), lambda i, be, nu: (i, 0)),
    )
    return pl.pallas_call(
        _expert_kernel,
        grid_spec=grid_spec,
        out_shape=jax.ShapeDtypeStruct((P * sub, LANE), F32),
        compiler_params=_cparams(("arbitrary",)),
        name="experts",
    )(blk_expert, n_used, xs, w_gate, w_up, w_down)


def _final_kernel(slot_ref, h_ref, gts_ref, g_ref, ys_ref, o_ref, buf_ref, sem, *, apply_norm):
    n_tok, d = h_ref.shape
    sub = d // LANE
    tile = lambda ref, r: ref.at[pl.ds(pl.multiple_of(r * sub, sub), sub), :]

    def start(t, c):
        for j in range(2):
            pltpu.make_async_copy(tile(ys_ref, slot_ref[0, 0, 2 * t + j]), tile(buf_ref.at[j], t), sem).start()
        return c

    lax.fori_loop(0, n_tok, start, 0)
    for j in range(2):
        pltpu.make_async_copy(ys_ref.at[pl.ds(0, n_tok * sub), :], buf_ref.at[j], sem).wait()
    y = (gts_ref[:, 0:1] * _load_row_tiles(buf_ref.at[0], n_tok, sub)
         + gts_ref[:, 1:2] * _load_row_tiles(buf_ref.at[1], n_tok, sub))
    h = h_ref[...] + y
    o_ref[...] = _rms(h, g_ref[...]) if apply_norm else h


def _final(h2, ys, slots, gts, norm_final, apply_norm):
    T, D = h2.shape
    tm = ROW_TOK
    row = lambda n: pl.BlockSpec((tm, n), lambda i: (i, 0))
    return pl.pallas_call(
        functools.partial(_final_kernel, apply_norm=apply_norm),
        grid=(T // tm,),
        in_specs=[pl.BlockSpec((1, 1, 2 * tm), lambda i: (i, 0, 0), memory_space=pltpu.SMEM),
                  row(D), row(LANE), _const_spec((1, D)), pl.BlockSpec(memory_space=pl.ANY)],
        out_specs=row(D),
        out_shape=jax.ShapeDtypeStruct((T, D), F32),
        scratch_shapes=[pltpu.VMEM((2, tm * (D // LANE), LANE), F32), pltpu.SemaphoreType.DMA(())],
        compiler_params=_cparams(("arbitrary",)),
        name="final",
    )(slots.reshape(T // tm, 1, 2 * tm), h2, gts, norm_final.reshape(1, D), ys)


def _pad_cols(w, n):
    return jnp.pad(w, ((0, 0), (0, n - w.shape[1])))


def _pad_rows(w, n):
    return jnp.pad(w, ((0, n - w.shape[0]), (0, 0)))


def _layer(h, mem, norm_mix, w_in, shift_mu, rw_w0, rw_w2, rw_a0, rw_a2, rw_g2, rw_k_k, rw_k_a,
           rw_r_k, rw_ln_w, rw_ln_b, kv_norm, w_uk, w_uv, w_proj_a, w_proj_b, b_gate, w_out,
           norm_cross, norm_mem, w_cq, w_ckv, w_co, norm_ffn, w_router_g, b_router_g, w_router_e,
           b_router_e, w_e_gate, w_e_up, w_e_down):
    B, S, D = h.shape
    M = mem.shape[1]
    T = B * S
    W = RW_WIDTH
    x2 = h.reshape(T, D)

    c = 0
    pieces = {}
    for name, n in (("r", W), ("k", W), ("v", W), ("xw", RW_DECAY_LORA), ("xa", RW_AAA_LORA),
                    ("xg", RW_GATE_LORA), ("q", DSA_WIDTH), ("ckv", DSA_KV_RANK),
                    ("qidx", IDX_HEADS * IDX_DIM), ("kidx", IDX_DIM), ("widx", IDX_HEADS),
                    ("gate", 2 * D)):
        pieces[name] = (c, n)
        c += n
    col = lambda name: w_in[:, pieces[name][0]:pieces[name][0] + pieces[name][1]]
    mu_of = lambda name: shift_mu[pieces[name][0]:pieces[name][0] + pieces[name][1]]
    w_rw = jnp.concatenate([col("r"), col("k"), col("v"), _pad_cols(col("xw"), LANE),
                            _pad_cols(col("xa"), LANE), _pad_cols(col("xg"), LANE)], axis=1).astype(BF16)
    padv = lambda t: jnp.pad(t, (0, LANE - t.shape[0]))
    mu = jnp.concatenate([mu_of("r"), mu_of("k"), mu_of("v"), padv(mu_of("xw")), padv(mu_of("xa")),
                          padv(mu_of("xg"))])
    qidx_w = col("qidx").reshape(D, IDX_HEADS, IDX_DIM)
    qidx_w = jnp.pad(qidx_w, ((0, 0), (0, 0), (0, LANE - IDX_DIM))).reshape(D, IDX_HEADS * LANE)
    w_dsa = jnp.concatenate([col("q"), col("ckv"), qidx_w, _pad_cols(col("kidx"), LANE),
                             _pad_cols(col("widx"), LANE)], axis=1).astype(BF16)
    w_gate_in = col("gate").astype(BF16)
    eye_h = jnp.eye(DSA_HEADS, dtype=F32)
    wuk_bd = jnp.einsum("rhd,hg->hdgr", w_uk, eye_h).reshape(DSA_WIDTH, DSA_HEADS * DSA_KV_RANK).astype(BF16)
    wuv_bd = jnp.einsum("rhd,hg->hrgd", w_uv, eye_h).reshape(DSA_HEADS * DSA_KV_RANK, DSA_WIDTH).astype(BF16)

    tm1 = min(256, T)
    z_rw, qlat, ckv, qidx, kidx, widx, gates = _inproj(
        x2, norm_mix, w_rw, w_dsa, w_gate_in, b_gate, wuk_bd, kv_norm, tm1)

    y_a = _rwkv(z_rw, B, S, mu, rw_w0, _pad_rows(rw_w2, LANE).astype(BF16), rw_a0,
                _pad_rows(rw_a2, LANE).astype(BF16), rw_g2.astype(BF16), rw_k_k, rw_k_a, rw_r_k,
                rw_ln_w, rw_ln_b)
    y_b = _dsa(qlat, qidx, widx, ckv, kidx, wuv_bd, B, S)
    kv = _memkv(mem.reshape(B * M, D), norm_mem, w_ckv.astype(BF16), M)

    w_r = jnp.concatenate([w_router_g, w_router_e], axis=1)
    w_r = _pad_cols(w_r, LANE)
    wr_hi = w_r.astype(BF16)
    wr_lo = (w_r - wr_hi.astype(F32)).astype(BF16)
    b_r = jnp.pad(jnp.concatenate([b_router_g, b_router_e]), (0, LANE - N_GROUPS - N_EXPERTS)).reshape(1, LANE)
    tm5 = min(256, S)
    h2, hn3, ids, gts = _mix(x2, y_a, y_b, gates, kv, w_proj_a.astype(BF16), w_proj_b.astype(BF16),
                             w_out.astype(BF16), norm_cross, w_cq.astype(BF16), w_co.astype(BF16),
                             norm_ffn, wr_hi, wr_lo, b_r, B, S, M, tm5)

    A = 2 * T
    e_flat = ids[:, :2].reshape(A)
    onehot = (e_flat[:, None] == jnp.arange(N_EXPERTS, dtype=I32)[None, :]).astype(I32)
    csum = jnp.cumsum(onehot, axis=0)
    rank = jnp.sum(csum * onehot, axis=1) - 1
    counts = csum[-1]
    padded = (counts + MOE_BLOCK - 1) // MOE_BLOCK * MOE_BLOCK
    pend = jnp.cumsum(padded)
    pstart = pend - padded
    slots = (jnp.sum(pstart[None, :] * onehot, axis=1) + rank).astype(I32)
    P = A + N_EXPERTS * MOE_BLOCK
    nb = P // MOE_BLOCK
    blk_pos = jnp.arange(nb, dtype=I32) * MOE_BLOCK
    blk_expert = jnp.minimum(jnp.sum((pend[None, :] <= blk_pos[:, None]).astype(I32), axis=1),
                             N_EXPERTS - 1).astype(I32)
    n_used = (pend[-1:] // MOE_BLOCK).astype(I32)

    xs = _dispatch(slots, hn3, P, D // LANE)
    ys = _experts(xs, blk_expert, n_used, w_e_gate.astype(BF16), w_e_up.astype(BF16),
                  w_e_down.astype(BF16))
    return h2, ys, slots, gts


def kernel(x, mem, norm_mix, w_in, shift_mu, rw_w0, rw_w2, rw_a0, rw_a2, rw_g2, rw_k_k, rw_k_a, rw_r_k, rw_ln_w, rw_ln_b, kv_norm, w_uk, w_uv, w_proj_a, w_proj_b, b_gate, w_out, norm_cross, norm_mem, w_cq, w_ckv, w_co, norm_ffn, w_router_g, b_router_g, w_router_e, b_router_e, w_e_gate, w_e_up, w_e_down, norm_final):
    B, S, D = x.shape
    depth = norm_mix.shape[0]
    h = x
    for l in range(depth):
        last = l == depth - 1
        h2, ys, slots, gts = _layer(
            h, mem, norm_mix[l], w_in[l], shift_mu[l], rw_w0[l], rw_w2[l], rw_a0[l], rw_a2[l],
            rw_g2[l], rw_k_k[l], rw_k_a[l], rw_r_k[l], rw_ln_w[l], rw_ln_b[l], kv_norm[l], w_uk[l],
            w_uv[l], w_proj_a[l], w_proj_b[l], b_gate[l], w_out[l], norm_cross[l], norm_mem[l],
            w_cq[l], w_ckv[l], w_co[l], norm_ffn[l], w_router_g[l], b_router_g[l], w_router_e[l],
            b_router_e[l], w_e_gate[l], w_e_up[l], w_e_down[l])
        h = _final(h2, ys, slots, gts, norm_final, last).reshape(B, S, D)
    return h
```

```python
import functools

import jax
import jax.numpy as jnp
from jax import lax
from jax.experimental import pallas as pl
from jax.experimental.pallas import tpu as pltpu

F32 = jnp.float32
BF16 = jnp.bfloat16
I32 = jnp.int32

RMS_EPS = 1e-6
RW_HEADS = 8
RW_HEAD_DIM = 64
RW_WIDTH = RW_HEADS * RW_HEAD_DIM
RW_DECAY_LORA = 64
RW_AAA_LORA = 64
RW_GATE_LORA = 128
RW_GN_EPS = 64e-5
DSA_HEADS = 8
DSA_HEAD_DIM = 64
DSA_WIDTH = DSA_HEADS * DSA_HEAD_DIM
DSA_KV_RANK = 128
IDX_HEADS = 4
IDX_DIM = 64
IDX_TOPK_MAX = 256
X_HEADS = 4
N_GROUPS = 4
EXPERTS_PER_GROUP = 8
N_EXPERTS = N_GROUPS * EXPERTS_PER_GROUP
MOE_BLOCK = 256

LANE = 128
RW_CHUNK = 64
RW_SEQS = 4
RW_ZCOLS = 3 * RW_WIDTH + 3 * LANE
DSA_ZCOLS = DSA_WIDTH + DSA_KV_RANK + IDX_HEADS * LANE + LANE + LANE
DSA_TQ = 512
DSA_TK = 256
ROW_BLOCK = 512
ROW_TOK = 256
NEG = -1e30
LOG2E = 1.4426950408889634
INT_MIN = -(2 ** 31)
VMEM_LIMIT = 56 * 1024 * 1024


def _mm(a, b):
    return jnp.dot(a.astype(BF16), b.astype(BF16), preferred_element_type=F32)


def _mm_nt(a, b):
    return lax.dot_general(a.astype(BF16), b.astype(BF16), (((1,), (1,)), ((), ())),
                           preferred_element_type=F32)


def _mm_tn(a, b):
    return lax.dot_general(a.astype(BF16), b.astype(BF16), (((0,), (0,)), ((), ())),
                           preferred_element_type=F32)


def _split3(x):
    hi = x.astype(BF16)
    r1 = x - hi.astype(F32)
    mid = r1.astype(BF16)
    lo = (r1 - mid.astype(F32)).astype(BF16)
    return hi, mid, lo


def _mm_hilo_rhs(x, b):
    hi = x.astype(BF16)
    lo = (x - hi.astype(F32)).astype(BF16)
    return jnp.dot(hi, b, preferred_element_type=F32) + jnp.dot(lo, b, preferred_element_type=F32)


def _mm_exact_lhs(a, x):
    hi, mid, lo = _split3(x)
    return (jnp.dot(a, hi, preferred_element_type=F32) + jnp.dot(a, mid, preferred_element_type=F32)
            + jnp.dot(a, lo, preferred_element_type=F32))


def _rms(x, g):
    return x * lax.rsqrt(jnp.mean(x * x, axis=-1, keepdims=True) + RMS_EPS) * g


def _sigmoid(x):
    return 1.0 / (1.0 + jnp.exp(-x))


def _softplus(x):
    return jnp.maximum(x, 0.0) + jnp.log1p(jnp.exp(-jnp.abs(x)))


def _load_row_tiles(ref, rows, sub):
    return jnp.concatenate([ref[pl.ds(c, rows, stride=sub), :] for c in range(sub)], axis=-1)


def _store_row_tiles(ref, val):
    rows, d = val.shape
    sub = d // LANE
    for c in range(sub):
        ref[pl.ds(c, rows, stride=sub), :] = val[:, c * LANE:(c + 1) * LANE]


def _cparams(sem):
    return pltpu.CompilerParams(dimension_semantics=sem, vmem_limit_bytes=VMEM_LIMIT)


def _const_spec(shape):
    nd = len(shape)
    return pl.BlockSpec(shape, lambda *_: (0,) * nd, pipeline_mode=pl.Buffered(1))


def _inproj_kernel(x_ref, g_ref, wrw_ref, wdsa_ref, wgate_ref, bgate_ref, wuk_ref, kvn_ref,
                   zrw_ref, qlat_ref, ckv_ref, qidx_ref, kidx_ref, widx_ref, gate_ref):
    hn = _rms(x_ref[...], g_ref[...]).astype(BF16)
    zrw_ref[...] = jnp.dot(hn, wrw_ref[...], preferred_element_type=F32)
    zd = jnp.dot(hn, wdsa_ref[...], preferred_element_type=F32)
    o = 0
    q = zd[:, o:o + DSA_WIDTH]; o += DSA_WIDTH
    c = zd[:, o:o + DSA_KV_RANK]; o += DSA_KV_RANK
    qi = zd[:, o:o + IDX_HEADS * LANE]; o += IDX_HEADS * LANE
    ki = zd[:, o:o + LANE]; o += LANE
    wi = zd[:, o:o + LANE]
    qlat_ref[...] = (_mm(q, wuk_ref[...]) * (DSA_HEAD_DIM ** -0.5 * LOG2E)).astype(BF16)
    cn = _rms(c, kvn_ref[...]).astype(BF16)
    ckv_ref[...] = jnp.concatenate([cn, jnp.ones_like(cn)], axis=-1)
    qidx_ref[...] = qi.astype(BF16)
    kidx_ref[...] = ki.astype(BF16)
    widx_ref[...] = wi * ((IDX_HEADS * IDX_DIM) ** -0.5)
    zg = jnp.dot(hn, wgate_ref[...], preferred_element_type=F32) + bgate_ref[...]
    gate_ref[...] = _sigmoid(zg)


def _inproj(x2, norm_mix, w_rw, w_dsa, w_gate, b_gate, wuk_bd, kv_norm, tm):
    T, D = x2.shape
    grid = (T // tm,)
    row = lambda n: pl.BlockSpec((tm, n), lambda i: (i, 0))
    outs = [
        jax.ShapeDtypeStruct((T, RW_ZCOLS), F32),
        jax.ShapeDtypeStruct((T, DSA_HEADS * DSA_KV_RANK), BF16),
        jax.ShapeDtypeStruct((T, 2 * DSA_KV_RANK), BF16),
        jax.ShapeDtypeStruct((T, IDX_HEADS * LANE), BF16),
        jax.ShapeDtypeStruct((T, LANE), BF16),
        jax.ShapeDtypeStruct((T, LANE), F32),
        jax.ShapeDtypeStruct((T, 2 * D), F32),
    ]
    return pl.pallas_call(
        _inproj_kernel,
        grid=grid,
        in_specs=[row(D), _const_spec((1, D)), _const_spec(w_rw.shape), _const_spec(w_dsa.shape),
                  _const_spec(w_gate.shape), _const_spec((1, 2 * D)), _const_spec(wuk_bd.shape),
                  _const_spec((1, DSA_KV_RANK))],
        out_specs=[row(s.shape[1]) for s in outs],
        out_shape=outs,
        compiler_params=_cparams(("parallel",)),
        name="inproj",
    )(x2, norm_mix.reshape(1, D), w_rw, w_dsa, w_gate, b_gate.reshape(1, 2 * D), wuk_bd,
      kv_norm.reshape(1, DSA_KV_RANK))


def _rwkv_kernel(z_ref, mu_ref, w0_ref, w2_ref, a0_ref, a2_ref, g2_ref, kk_ref, ka_ref, rk_ref,
                 lnw_ref, lnb_ref, hs_ref, y_ref, state_ref, carry_ref, ybuf_ref):
    C, N, H = RW_CHUNK, RW_HEAD_DIM, RW_HEADS
    W = RW_WIDTH
    NB = z_ref.shape[0]

    @pl.when(pl.program_id(1) == 0)
    def _():
        state_ref[...] = jnp.zeros_like(state_ref)
        carry_ref[...] = jnp.zeros_like(carry_ref)

    hs = hs_ref[...]
    headsum = lambda t: _mm_hilo_rhs(t, hs)
    ti = lax.broadcasted_iota(I32, (C, C), 0)
    si = lax.broadcasted_iota(I32, (C, C), 1)
    ltri = (si <= ti).astype(BF16)
    t2 = lax.broadcasted_iota(I32, (C, 2 * C), 0)
    s2 = lax.broadcasted_iota(I32, (C, 2 * C), 1) % C
    m_strict = s2 < t2
    m_incl = s2 <= t2
    eye = (si == ti).astype(F32)
    zeros_cn = jnp.zeros((C, N), F32)
    row = lax.broadcasted_iota(I32, (C, RW_ZCOLS), 0)

    a_t, r_t, b_t, k_t, v, p_c, bonus, g = ([None] * NB for _ in range(8))
    for bi in range(NB):
        z = z_ref[bi]
        zprev = jnp.where(row == 0, carry_ref[bi], pltpu.roll(z, 1, 0))
        carry_ref[bi] = z[C - 1:C, :]
        zm = z + (zprev - z) * mu_ref[...]
        r = zm[:, 0:W]
        k = zm[:, W:2 * W]
        v[bi] = zm[:, 2 * W:3 * W]
        xw = zm[:, 3 * W:3 * W + LANE]
        xa = zm[:, 3 * W + LANE:3 * W + 2 * LANE]
        xg = zm[:, 3 * W + 2 * LANE:3 * W + 3 * LANE]
        wlog = -_softplus(-(w0_ref[...] + _mm(jnp.tanh(xw), w2_ref[...]))) - 0.5
        dlog = -jnp.exp(wlog)
        a = _sigmoid(a0_ref[...] + _mm(xa, a2_ref[...]))
        g[bi] = _mm(_sigmoid(xg), g2_ref[...])
        kk = k * kk_ref[...]
        kk = kk / jnp.maximum(jnp.sqrt(headsum(kk * kk)), 1e-12)
        k_eff = k * (1.0 + (a - 1.0) * ka_ref[...])
        bonus[bi] = headsum(r * k_eff * rk_ref[...]) * v[bi]
        cum = _mm_exact_lhs(ltri, dlog)
        pin = jnp.exp(cum)
        pinv = jnp.exp(-cum)
        a_t[bi] = -kk * jnp.exp(cum - dlog)
        r_t[bi] = r * pin
        b_t[bi] = kk * a * pinv
        k_t[bi] = k_eff * pinv
        p_c[bi] = pin[C - 1:C, :]

    units = [(bi, h) for bi in range(NB) for h in range(H)]
    ur = range(len(units))
    sl = lambda h: slice(h * N, (h + 1) * N)
    ar = [jnp.concatenate([a_t[bi][:, sl(h)], r_t[bi][:, sl(h)]], axis=0).astype(BF16) for bi, h in units]
    bk = [jnp.concatenate([b_t[bi][:, sl(h)], k_t[bi][:, sl(h)]], axis=0).astype(BF16) for bi, h in units]
    vh = [v[bi][:, sl(h)] for bi, h in units]
    s0 = [state_ref[bi, h] for bi, h in units]
    q = [_mm_nt(ar[u], bk[u]) for u in ur]
    w_all = [_mm_nt(ar[u], s0[u]) for u in ur]
    top = [jnp.where(m_strict, q[u][:C], 0.0).astype(BF16) for u in ur]
    bot = [jnp.where(m_incl, q[u][C:], 0.0).astype(BF16) for u in ur]
    rhs = [w_all[u][:C] + _mm(top[u], jnp.concatenate([zeros_cn, vh[u]], axis=0)) for u in ur]
    pw = [top[u][:, :C] for u in ur]
    inv = [eye + pw[u].astype(F32) for u in ur]
    for _ in range(5):
        pw = [_mm(pw[u], pw[u]).astype(BF16) for u in ur]
        inv = [inv[u] + _mm(pw[u], inv[u]) for u in ur]
    uu = [_mm(inv[u], rhs[u]) for u in ur]
    gsrc = [jnp.concatenate([uu[u], vh[u]], axis=0).astype(BF16) for u in ur]
    for u, (bi, h) in enumerate(units):
        ybuf_ref[bi, :, sl(h)] = w_all[u][C:] + _mm(bot[u], gsrc[u])
    for u, (bi, h) in enumerate(units):
        state_ref[bi, h] = (s0[u] + _mm_tn(gsrc[u], bk[u])) * p_c[bi][:, sl(h)]

    for bi in range(NB):
        y = ybuf_ref[bi]
        mean = headsum(y) * (1.0 / N)
        d = y - mean
        var = headsum(d * d) * (1.0 / N)
        yn = d * lax.rsqrt(var + RW_GN_EPS) * lnw_ref[...] + lnb_ref[...]
        y_ref[bi] = ((yn + bonus[bi]) * g[bi]).astype(y_ref.dtype)


def _rwkv(z_rw, B, S, mu, w0, w2, a0, a2, g2, k_k, k_a, r_k, ln_w, ln_b):
    C, W = RW_CHUNK, RW_WIDTH
    nb = RW_SEQS if B % RW_SEQS == 0 else 1
    nc = S // C
    hid = jnp.arange(W) // RW_HEAD_DIM
    hs = (hid[:, None] == hid[None, :]).astype(BF16)
    vec = lambda t: t.reshape(1, -1)
    consts = [vec(mu), vec(w0), w2, vec(a0), a2, g2, vec(k_k), vec(k_a), vec(r_k), vec(ln_w),
              vec(ln_b), hs]
    y = pl.pallas_call(
        _rwkv_kernel,
        grid=(B // nb, nc),
        in_specs=[pl.BlockSpec((nb, C, RW_ZCOLS), lambda b, c: (b, c, 0))]
                 + [_const_spec(t.shape) for t in consts],
        out_specs=pl.BlockSpec((nb, C, W), lambda b, c: (b, c, 0)),
        out_shape=jax.ShapeDtypeStruct((B, S, W), BF16),
        scratch_shapes=[pltpu.VMEM((nb, RW_HEADS, RW_HEAD_DIM, RW_HEAD_DIM), F32),
                        pltpu.VMEM((nb, 1, RW_ZCOLS), F32),
                        pltpu.VMEM((nb, C, W), F32)],
        compiler_params=_cparams(("parallel", "arbitrary")),
        name="rwkv",
    )(z_rw.reshape(B, S, RW_ZCOLS), *consts)
    return y.reshape(B * S, W)


def _dsa_kernel(qlat_ref, qidx_ref, widx_ref, ckv_ref, kidx_ref, wuv_ref, y_ref,
                keys_ref, hi_ref, lo_ref, thr_ref, need_ref, eqc_ref, bias_ref, m_ref, acc_ref, *, k_sel):
    TQ, TK = DSA_TQ, DSA_TK
    RC = 128
    HQ = 128
    I16 = jnp.int16
    HALF = 1 << 15
    qi = pl.program_id(1)
    nkt = (qi + 1) * (TQ // TK)
    npair = nkt // 2
    rowpos = qi * TQ + lax.broadcasted_iota(I32, (TQ, TK), 0)
    lane = lax.broadcasted_iota(I32, (TQ, TK), 1)
    ones_ll = jnp.ones((LANE, LANE), BF16)
    rep2 = lambda t: jnp.concatenate([t, t], axis=-1)

    def score_body(kt, carry):
        kx = kidx_ref[pl.ds(pl.multiple_of(kt * TK, TK), TK), :]
        sc = jnp.zeros((TQ, TK), F32)
        for h in range(IDX_HEADS):
            rel = jnp.maximum(_mm_nt(qidx_ref[:, h * LANE:(h + 1) * LANE], kx), 0.0)
            sc = sc + widx_ref[:, h:h + 1] * rel
        sc = jnp.where(sc == 0.0, 0.0, sc)
        bits = pltpu.bitcast(sc, I32)
        key = jnp.where(bits >= 0, bits, bits ^ jnp.int32(0x7FFFFFFF))
        key = jnp.where(kt * TK + lane <= rowpos, key, jnp.int32(INT_MIN))
        keys_ref[kt] = key
        hi_ref[kt] = (key >> 16).astype(I16)
        return carry

    lax.fori_loop(0, nkt, score_body, 0)

    def count16(ref, level, strict):
        accs = []
        for rc in range(TQ // RC):
            lv = level[rc * RC:(rc + 1) * RC].astype(I16)

            def body(kp, acc, rc=rc, lv=lv):
                for j in range(2):
                    for half in range(TK // LANE):
                        kk = ref[2 * kp + j, rc * RC:(rc + 1) * RC, half * LANE:(half + 1) * LANE]
                        acc = acc + ((kk > lv) if strict else (kk >= lv)).astype(I16)
                return acc

            accs.append(lax.fori_loop(0, npair, body, jnp.zeros((RC, LANE), I16)))
        acc = jnp.concatenate(accs, axis=0).astype(I32).astype(F32).astype(BF16)
        return jnp.dot(acc, ones_ll, preferred_element_type=F32)

    def select16(ref, k_row):
        def bit_body(i, tu):
            cand = tu | lax.shift_left(jnp.int32(1), 15 - i)
            return jnp.where(count16(ref, cand - HALF, strict=False) >= k_row, cand, tu)
        return lax.fori_loop(0, 16, bit_body, jnp.zeros((TQ, LANE), I32))

    th = select16(hi_ref, jnp.full((TQ, LANE), k_sel, F32)) - HALF
    n_gt_hi = count16(hi_ref, th, strict=True)
    thr_ref[...] = th

    def lo_body(kt, carry):
        key = keys_ref[kt]
        lo = (key & jnp.int32(0xFFFF)) - HALF
        lo_ref[kt] = jnp.where((key >> 16) == rep2(thr_ref[...]), lo, -HALF).astype(I16)
        return carry

    lax.fori_loop(0, nkt, lo_body, 0)
    tl = select16(lo_ref, k_sel - n_gt_hi)
    n_gt = n_gt_hi + count16(lo_ref, tl - HALF, strict=True)
    thr_ref[...] = th * (1 << 16) + tl
    need_ref[...] = k_sel - n_gt

    eqc_ref[...] = jnp.zeros_like(eqc_ref)
    m_ref[...] = jnp.full_like(m_ref, NEG)
    acc_ref[...] = jnp.zeros_like(acc_ref)
    ucol = lax.broadcasted_iota(I32, (TK, TK + LANE), 1)
    uext = ((lax.broadcasted_iota(I32, (TK, TK + LANE), 0) < ucol) | (ucol >= TK)).astype(BF16)
    R = DSA_KV_RANK

    def att_body(kt, carry):
        key = keys_ref[kt]
        thr = rep2(thr_ref[...])
        eq = key == thr
        pre = jnp.dot(eq.astype(F32).astype(BF16), uext, preferred_element_type=F32)
        eqc = eqc_ref[...]
        sel = ((key > thr) | (eq & (pre[:, :TK] + rep2(eqc) < rep2(need_ref[...])))) & (kt * TK + lane <= rowpos)
        bias_ref[...] = jnp.where(sel, 0.0, NEG)
        eqc_ref[...] = eqc + pre[:, TK:]
        ct = ckv_ref[pl.ds(pl.multiple_of(kt * TK, TK), TK), :]
        ck = ct[:, :R]
        for h in range(DSA_HEADS):
            for r0 in range(0, TQ, HQ):
                rows = pl.ds(r0, HQ)
                s = _mm_nt(qlat_ref[rows, h * R:(h + 1) * R], ck) + bias_ref[rows, :]
                m_old = m_ref[h, rows, :]
                s_max = jnp.max(jnp.maximum(s[:, :LANE], s[:, LANE:]), axis=-1, keepdims=True)
                m_new = jnp.maximum(m_old, s_max)
                p = jnp.exp2(s - rep2(m_new))
                acc_ref[h, rows, :] = (rep2(jnp.exp2(m_old - m_new)) * acc_ref[h, rows, :]
                                       + jnp.dot(p.astype(BF16), ct, preferred_element_type=F32))
                m_ref[h, rows, :] = m_new
        return carry

    lax.fori_loop(0, nkt, att_body, 0)
    o = jnp.concatenate([(acc_ref[h, :, :R] / acc_ref[h, :, R:]).astype(BF16)
                         for h in range(DSA_HEADS)], axis=-1)
    y_ref[...] = jnp.dot(o, wuv_ref[...], preferred_element_type=F32).astype(y_ref.dtype)


def _dsa(qlat, qidx, widx, ckv, kidx, wuv_bd, B, S):
    TQ, TK = DSA_TQ, DSA_TK
    assert TQ == 2 * TK and TK == 2 * LANE and S % TQ == 0
    nq = S // TQ
    nk = S // TK
    k_sel = min(IDX_TOPK_MAX, S // 4)
    qrow = lambda n: pl.BlockSpec((TQ, n), lambda b, q: (b * nq + q, 0))
    brow = lambda n: pl.BlockSpec((S, n), lambda b, q: (b, 0))
    return pl.pallas_call(
        functools.partial(_dsa_kernel, k_sel=k_sel),
        grid=(B, nq),
        in_specs=[qrow(DSA_HEADS * DSA_KV_RANK), qrow(IDX_HEADS * LANE), qrow(LANE),
                  brow(2 * DSA_KV_RANK), brow(LANE), _const_spec(wuv_bd.shape)],
        out_specs=qrow(DSA_WIDTH),
        out_shape=jax.ShapeDtypeStruct((B * S, DSA_WIDTH), BF16),
        scratch_shapes=[pltpu.VMEM((nk, TQ, TK), I32), pltpu.VMEM((nk, TQ, TK), jnp.int16),
                        pltpu.VMEM((nk, TQ, TK), jnp.int16),
                        pltpu.VMEM((TQ, LANE), I32), pltpu.VMEM((TQ, LANE), F32), pltpu.VMEM((TQ, LANE), F32),
                        pltpu.VMEM((TQ, TK), F32),
                        pltpu.VMEM((DSA_HEADS, TQ, LANE), F32),
                        pltpu.VMEM((DSA_HEADS, TQ, 2 * DSA_KV_RANK), F32)],
        compiler_params=_cparams(("parallel", "arbitrary")),
        name="dsa",
    )(qlat, qidx, widx, ckv, kidx, wuv_bd)


def _memkv_kernel(m_ref, g_ref, w_ref, o_ref):
    o_ref[...] = _mm(_rms(m_ref[...], g_ref[...]), w_ref[...]).astype(o_ref.dtype)


def _memkv(mem2, norm_mem, w_ckv, M):
    R, D = mem2.shape
    return pl.pallas_call(
        _memkv_kernel,
        grid=(R // M,),
        in_specs=[pl.BlockSpec((M, D), lambda i: (i, 0)), _const_spec((1, D)), _const_spec(w_ckv.shape)],
        out_specs=pl.BlockSpec((M, w_ckv.shape[1]), lambda i: (i, 0)),
        out_shape=jax.ShapeDtypeStruct((R, w_ckv.shape[1]), BF16),
        compiler_params=_cparams(("parallel",)),
        name="memkv",
    )(mem2, norm_mem.reshape(1, D), w_ckv)


def _mix_kernel(x_ref, ya_ref, yb_ref, gate_ref, kv_ref, wpa_ref, wpb_ref, wout_ref, ncross_ref,
                wcq_ref, wco_ref, nffn_ref, wrh_ref, wrl_ref, br_ref,
                h2_ref, hn3_ref, ids_ref, gts_ref):
    D = x_ref.shape[1]
    xw = D // X_HEADS
    a = jnp.dot(ya_ref[...], wpa_ref[...], preferred_element_type=F32)
    b = jnp.dot(yb_ref[...], wpb_ref[...], preferred_element_type=F32)
    mix = gate_ref[:, :D] * a + gate_ref[:, D:] * b
    h1 = x_ref[...] + _mm(mix, wout_ref[...])
    q = _mm(_rms(h1, ncross_ref[...]), wcq_ref[...]).astype(BF16)
    outs = []
    for h in range(X_HEADS):
        s = _mm_nt(q[:, h * xw:(h + 1) * xw], kv_ref[:, h * xw:(h + 1) * xw]) * (xw ** -0.5)
        e = jnp.exp(s - jnp.max(s, axis=-1, keepdims=True))
        p = e / jnp.sum(e, axis=-1, keepdims=True)
        outs.append(_mm(p, kv_ref[:, D + h * xw:D + (h + 1) * xw]).astype(BF16))
    h2 = h1 + jnp.dot(jnp.concatenate(outs, axis=-1), wco_ref[...], preferred_element_type=F32)
    h2_ref[...] = h2
    hn3 = _rms(h2, nffn_ref[...])
    _store_row_tiles(hn3_ref, hn3)

    hi = hn3.astype(BF16)
    lo = (hn3 - hi.astype(F32)).astype(BF16)
    lg = (jnp.dot(hi, wrh_ref[...], preferred_element_type=F32)
          + jnp.dot(lo, wrh_ref[...], preferred_element_type=F32)
          + jnp.dot(hi, wrl_ref[...], preferred_element_type=F32)) + br_ref[...]
    lane = lax.broadcasted_iota(I32, lg.shape, 1)
    big = jnp.int32(1 << 20)
    rmax = lambda t: jnp.max(t, axis=-1, keepdims=True)
    rsum = lambda t: jnp.sum(t, axis=-1, keepdims=True)
    first = lambda hit: jnp.min(jnp.where(hit, lane, big), axis=-1, keepdims=True)
    gmask = lane < N_GROUPS
    gl = jnp.where(gmask, lg, NEG)
    ge = jnp.where(gmask, jnp.exp(gl - rmax(gl)), 0.0)
    gp = jnp.where(gmask, ge / rsum(ge), -1.0)
    p_grp = rmax(gp)
    grp = first(gp == p_grp)
    eidx = lane - N_GROUPS
    emask = (eidx >= 0) & (eidx < N_EXPERTS) & ((eidx // EXPERTS_PER_GROUP) == grp)
    el = jnp.where(emask, lg, NEG)
    ee = jnp.where(emask, jnp.exp(el - rmax(el)), 0.0)
    ep = jnp.where(emask, ee / rsum(ee), -1.0)
    p1 = rmax(ep)
    i1 = first(ep == p1)
    ep2 = jnp.where(lane == i1, -1.0, ep)
    p2 = rmax(ep2)
    i2 = first(ep2 == p2)
    den = p1 + p2
    ids_ref[...] = jnp.where(lane == 0, i1 - N_GROUPS, jnp.where(lane == 1, i2 - N_GROUPS, 0))
    gts_ref[...] = jnp.where(lane == 0, p_grp * p1 / den, jnp.where(lane == 1, p_grp * p2 / den, 0.0))


def _mix(x2, ya, yb, gates, kv, wpa, wpb, wout, ncross, wcq, wco, nffn, wr_hi, wr_lo, br, B, S, M, tm):
    T, D = x2.shape
    nt = S // tm
    row = lambda n: pl.BlockSpec((tm, n), lambda i: (i, 0))
    consts = [wpa, wpb, wout, ncross.reshape(1, D), wcq, wco, nffn.reshape(1, D), wr_hi, wr_lo, br]
    outs = [jax.ShapeDtypeStruct((T, D), F32), jax.ShapeDtypeStruct((T * (D // LANE), LANE), F32),
            jax.ShapeDtypeStruct((T, LANE), I32), jax.ShapeDtypeStruct((T, LANE), F32)]
    return pl.pallas_call(
        _mix_kernel,
        grid=(T // tm,),
        in_specs=[row(D), row(RW_WIDTH), row(DSA_WIDTH), row(2 * D),
                  pl.BlockSpec((M, 2 * D), lambda i: (i // nt, 0))]
                 + [_const_spec(t.shape) for t in consts],
        out_specs=[row(D), pl.BlockSpec((tm * (D // LANE), LANE), lambda i: (i, 0)), row(LANE), row(LANE)],
        out_shape=outs,
        compiler_params=_cparams(("parallel",)),
        name="mix_cross_router",
    )(x2, ya, yb, gates, kv, *consts)


def _dispatch_kernel(slot_ref, src_ref, _, out_ref, sem, *, sub):
    n_tok = src_ref.shape[0] // sub
    tile = lambda ref, r: ref.at[pl.ds(pl.multiple_of(r * sub, sub), sub), :]

    def start(t, c):
        for j in range(2):
            pltpu.make_async_copy(tile(src_ref, t), tile(out_ref, slot_ref[0, 0, 2 * t + j]), sem).start()
        return c

    lax.fori_loop(0, n_tok, start, 0)
    for j in range(2):
        pltpu.make_async_copy(src_ref, out_ref.at[pl.ds(0, n_tok * sub), :], sem).wait()


def _dispatch(slots, src, out_rows, sub):
    T = src.shape[0] // sub
    td = ROW_TOK
    anyspec = pl.BlockSpec(memory_space=pl.ANY)
    return pl.pallas_call(
        functools.partial(_dispatch_kernel, sub=sub),
        grid=(T // td,),
        in_specs=[pl.BlockSpec((1, 1, 2 * td), lambda i: (i, 0, 0), memory_space=pltpu.SMEM),
                  pl.BlockSpec((td * sub, LANE), lambda i: (i, 0)), anyspec],
        out_specs=anyspec,
        out_shape=jax.ShapeDtypeStruct((out_rows * sub, LANE), src.dtype),
        scratch_shapes=[pltpu.SemaphoreType.DMA(())],
        input_output_aliases={2: 0},
        compiler_params=_cparams(("arbitrary",)),
        name="moe_dispatch",
    )(slots.reshape(T // td, 1, 2 * td), src, jnp.zeros((out_rows * sub, LANE), src.dtype))


def _expert_kernel(be_ref, nu_ref, x_ref, wg_ref, wu_ref, wd_ref, o_ref):
    i = pl.program_id(0)

    @pl.when(i < nu_ref[0])
    def _():
        sub = wg_ref.shape[0] // LANE
        x = _load_row_tiles(x_ref, MOE_BLOCK, sub).astype(BF16)
        gt = jnp.dot(x, wg_ref[...], preferred_element_type=F32)
        up = jnp.dot(x, wu_ref[...], preferred_element_type=F32)
        hid = gt * _sigmoid(gt) * up
        _store_row_tiles(o_ref, _mm(hid, wd_ref[...]))

    @pl.when(i >= nu_ref[0])
    def _():
        o_ref[...] = jnp.zeros_like(o_ref)


def _experts(xs, blk_expert, n_used, w_gate, w_up, w_down):
    E, D, F = w_gate.shape
    sub = D // LANE
    P = xs.shape[0] // sub
    nb = P // MOE_BLOCK
    grid_spec = pltpu.PrefetchScalarGridSpec(
        num_scalar_prefetch=2,
        grid=(nb,),
        in_specs=[pl.BlockSpec((MOE_BLOCK * sub, LANE), lambda i, be, nu: (i, 0)),
                  pl.BlockSpec((None, D, F), lambda i, be, nu: (be[i], 0, 0)),
                  pl.BlockSpec((None, D, F), lambda i, be, nu: (be[i], 0, 0)),
                  pl.BlockSpec((None, F, D), lambda i, be, nu: (be[i], 0, 0))],
        out_specs=pl.BlockSpec((MOE_BLOCK * sub, LANE), lambda i, be, nu: (i, 0)),
    )
    return pl.pallas_call(
        _expert_kernel,
        grid_spec=grid_spec,
        out_shape=jax.ShapeDtypeStruct((P * sub, LANE), F32),
        compiler_params=_cparams(("arbitrary",)),
        name="experts",
    )(blk_expert, n_used, xs, w_gate, w_up, w_down)


def _final_kernel(slot_ref, nslot_ref, h_ref, gts_ref, g_ref, ys_ref, o_ref, buf_ref, sem, *, apply_norm):
    n_tok, d = h_ref.shape
    sub = d // LANE
    tile = lambda ref, r: ref.at[pl.ds(pl.multiple_of(r * sub, sub), sub), :]
    i = pl.program_id(0)
    cur = i % 2

    def gather(s_ref, b):
        def start(t, c):
            for j in range(2):
                pltpu.make_async_copy(tile(ys_ref, s_ref[0, 0, 2 * t + j]), tile(buf_ref.at[b, j], t),
                                      sem.at[b]).start()
            return c
        lax.fori_loop(0, n_tok, start, 0)

    @pl.when(i == 0)
    def _():
        gather(slot_ref, 0)

    @pl.when(i + 1 < pl.num_programs(0))
    def _():
        gather(nslot_ref, 1 - cur)

    for j in range(2):
        pltpu.make_async_copy(ys_ref.at[pl.ds(0, n_tok * sub), :], buf_ref.at[cur, j], sem.at[cur]).wait()
    y = (gts_ref[:, 0:1] * _load_row_tiles(buf_ref.at[cur, 0], n_tok, sub)
         + gts_ref[:, 1:2] * _load_row_tiles(buf_ref.at[cur, 1], n_tok, sub))
    h = h_ref[...] + y
    o_ref[...] = _rms(h, g_ref[...]) if apply_norm else h


def _final(h2, ys, slots, gts, norm_final, apply_norm):
    T, D = h2.shape
    tm = ROW_TOK
    nt = T // tm
    row = lambda n: pl.BlockSpec((tm, n), lambda i: (i, 0))
    slots3 = slots.reshape(nt, 1, 2 * tm)
    return pl.pallas_call(
        functools.partial(_final_kernel, apply_norm=apply_norm),
        grid=(nt,),
        in_specs=[pl.BlockSpec((1, 1, 2 * tm), lambda i: (i, 0, 0), memory_space=pltpu.SMEM),
                  pl.BlockSpec((1, 1, 2 * tm), lambda i: (jnp.minimum(i + 1, nt - 1), 0, 0),
                               memory_space=pltpu.SMEM),
                  row(D), row(LANE), _const_spec((1, D)), pl.BlockSpec(memory_space=pl.ANY)],
        out_specs=row(D),
        out_shape=jax.ShapeDtypeStruct((T, D), F32),
        scratch_shapes=[pltpu.VMEM((2, 2, tm * (D // LANE), LANE), F32), pltpu.SemaphoreType.DMA((2,))],
        compiler_params=_cparams(("arbitrary",)),
        name="final",
    )(slots3, slots3, h2, gts, norm_final.reshape(1, D), ys)


def _pad_cols(w, n):
    return jnp.pad(w, ((0, 0), (0, n - w.shape[1])))


def _pad_rows(w, n):
    return jnp.pad(w, ((0, n - w.shape[0]), (0, 0)))


def _layer(h, mem, norm_mix, w_in, shift_mu, rw_w0, rw_w2, rw_a0, rw_a2, rw_g2, rw_k_k, rw_k_a,
           rw_r_k, rw_ln_w, rw_ln_b, kv_norm, w_uk, w_uv, w_proj_a, w_proj_b, b_gate, w_out,
           norm_cross, norm_mem, w_cq, w_ckv, w_co, norm_ffn, w_router_g, b_router_g, w_router_e,
           b_router_e, w_e_gate, w_e_up, w_e_down):
    B, S, D = h.shape
    M = mem.shape[1]
    T = B * S
    W = RW_WIDTH
    x2 = h.reshape(T, D)

    c = 0
    pieces = {}
    for name, n in (("r", W), ("k", W), ("v", W), ("xw", RW_DECAY_LORA), ("xa", RW_AAA_LORA),
                    ("xg", RW_GATE_LORA), ("q", DSA_WIDTH), ("ckv", DSA_KV_RANK),
                    ("qidx", IDX_HEADS * IDX_DIM), ("kidx", IDX_DIM), ("widx", IDX_HEADS),
                    ("gate", 2 * D)):
        pieces[name] = (c, n)
        c += n
    col = lambda name: w_in[:, pieces[name][0]:pieces[name][0] + pieces[name][1]]
    mu_of = lambda name: shift_mu[pieces[name][0]:pieces[name][0] + pieces[name][1]]
    w_rw = jnp.concatenate([col("r"), col("k"), col("v"), _pad_cols(col("xw"), LANE),
                            _pad_cols(col("xa"), LANE), _pad_cols(col("xg"), LANE)], axis=1).astype(BF16)
    padv = lambda t: jnp.pad(t, (0, LANE - t.shape[0]))
    mu = jnp.concatenate([mu_of("r"), mu_of("k"), mu_of("v"), padv(mu_of("xw")), padv(mu_of("xa")),
                          padv(mu_of("xg"))])
    qidx_w = col("qidx").reshape(D, IDX_HEADS, IDX_DIM)
    qidx_w = jnp.pad(qidx_w, ((0, 0), (0, 0), (0, LANE - IDX_DIM))).reshape(D, IDX_HEADS * LANE)
    w_dsa = jnp.concatenate([col("q"), col("ckv"), qidx_w, _pad_cols(col("kidx"), LANE),
                             _pad_cols(col("widx"), LANE)], axis=1).astype(BF16)
    w_gate_in = col("gate").astype(BF16)
    eye_h = jnp.eye(DSA_HEADS, dtype=F32)
    wuk_bd = jnp.einsum("rhd,hg->hdgr", w_uk, eye_h).reshape(DSA_WIDTH, DSA_HEADS * DSA_KV_RANK).astype(BF16)
    wuv_bd = jnp.einsum("rhd,hg->hrgd", w_uv, eye_h).reshape(DSA_HEADS * DSA_KV_RANK, DSA_WIDTH).astype(BF16)

    tm1 = min(ROW_BLOCK, T)
    z_rw, qlat, ckv, qidx, kidx, widx, gates = _inproj(
        x2, norm_mix, w_rw, w_dsa, w_gate_in, b_gate, wuk_bd, kv_norm, tm1)

    y_a = _rwkv(z_rw, B, S, mu, rw_w0, _pad_rows(rw_w2, LANE).astype(BF16), rw_a0,
                _pad_rows(rw_a2, LANE).astype(BF16), rw_g2.astype(BF16), rw_k_k, rw_k_a, rw_r_k,
                rw_ln_w, rw_ln_b)
    y_b = _dsa(qlat, qidx, widx, ckv, kidx, wuv_bd, B, S)
    kv = _memkv(mem.reshape(B * M, D), norm_mem, w_ckv.astype(BF16), M)

    w_r = jnp.concatenate([w_router_g, w_router_e], axis=1)
    w_r = _pad_cols(w_r, LANE)
    wr_hi = w_r.astype(BF16)
    wr_lo = (w_r - wr_hi.astype(F32)).astype(BF16)
    b_r = jnp.pad(jnp.concatenate([b_router_g, b_router_e]), (0, LANE - N_GROUPS - N_EXPERTS)).reshape(1, LANE)
    tm5 = min(ROW_BLOCK, S)
    h2, hn3, ids, gts = _mix(x2, y_a, y_b, gates, kv, w_proj_a.astype(BF16), w_proj_b.astype(BF16),
                             w_out.astype(BF16), norm_cross, w_cq.astype(BF16), w_co.astype(BF16),
                             norm_ffn, wr_hi, wr_lo, b_r, B, S, M, tm5)

    A = 2 * T
    e_flat = ids[:, :2].reshape(A)
    onehot = (e_flat[:, None] == jnp.arange(N_EXPERTS, dtype=I32)[None, :]).astype(I32)
    csum = jnp.cumsum(onehot, axis=0)
    rank = jnp.sum(csum * onehot, axis=1) - 1
    counts = csum[-1]
    padded = (counts + MOE_BLOCK - 1) // MOE_BLOCK * MOE_BLOCK
    pend = jnp.cumsum(padded)
    pstart = pend - padded
    slots = (jnp.sum(pstart[None, :] * onehot, axis=1) + rank).astype(I32)
    P = A + N_EXPERTS * MOE_BLOCK
    nb = P // MOE_BLOCK
    blk_pos = jnp.arange(nb, dtype=I32) * MOE_BLOCK
    blk_expert = jnp.minimum(jnp.sum((pend[None, :] <= blk_pos[:, None]).astype(I32), axis=1),
                             N_EXPERTS - 1).astype(I32)
    n_used = (pend[-1:] // MOE_BLOCK).astype(I32)

    xs = _dispatch(slots, hn3, P, D // LANE)
    ys = _experts(xs, blk_expert, n_used, w_e_gate.astype(BF16), w_e_up.astype(BF16),
                  w_e_down.astype(BF16))
    return h2, ys, slots, gts


def kernel(x, mem, norm_mix, w_in, shift_mu, rw_w0, rw_w2, rw_a0, rw_a2, rw_g2, rw_k_k, rw_k_a, rw_r_k, rw_ln_w, rw_ln_b, kv_norm, w_uk, w_uv, w_proj_a, w_proj_b, b_gate, w_out, norm_cross, norm_mem, w_cq, w_ckv, w_co, norm_ffn, w_router_g, b_router_g, w_router_e, b_router_e, w_e_gate, w_e_up, w_e_down, norm_final):
    B, S, D = x.shape
    depth = norm_mix.shape[0]
    h = x
    for l in range(depth):
        last = l == depth - 1
        h2, ys, slots, gts = _layer(
            h, mem, norm_mix[l], w_in[l], shift_mu[l], rw_w0[l], rw_w2[l], rw_a0[l], rw_a2[l],
            rw_g2[l], rw_k_k[l], rw_k_a[l], rw_r_k[l], rw_ln_w[l], rw_ln_b[l], kv_norm[l], w_uk[l],
            w_uv[l], w_proj_a[l], w_proj_b[l], b_gate[l], w_out[l], norm_cross[l], norm_mem[l],
            w_cq[l], w_ckv[l], w_co[l], norm_ffn[l], w_router_g[l], b_router_g[l], w_router_e[l],
            b_router_e[l], w_e_gate[l], w_e_up[l], w_e_down[l])
        h = _final(h2, ys, slots, gts, norm_final, last).reshape(B, S, D)
    return h
```

```python
import functools

import jax
import jax.numpy as jnp
from jax import lax
from jax.experimental import pallas as pl
from jax.experimental.pallas import tpu as pltpu

F32 = jnp.float32
BF16 = jnp.bfloat16
I32 = jnp.int32

RMS_EPS = 1e-6
RW_HEADS = 8
RW_HEAD_DIM = 64
RW_WIDTH = RW_HEADS * RW_HEAD_DIM
RW_DECAY_LORA = 64
RW_AAA_LORA = 64
RW_GATE_LORA = 128
RW_GN_EPS = 64e-5
DSA_HEADS = 8
DSA_HEAD_DIM = 64
DSA_WIDTH = DSA_HEADS * DSA_HEAD_DIM
DSA_KV_RANK = 128
IDX_HEADS = 4
IDX_DIM = 64
IDX_TOPK_MAX = 256
X_HEADS = 4
N_GROUPS = 4
EXPERTS_PER_GROUP = 8
N_EXPERTS = N_GROUPS * EXPERTS_PER_GROUP
MOE_BLOCK = 256

LANE = 128
RW_CHUNK = 64
RW_SEQS = 4
RW_ZCOLS = 3 * RW_WIDTH + 3 * LANE
DSA_ZCOLS = DSA_WIDTH + DSA_KV_RANK + IDX_HEADS * LANE + LANE + LANE
DSA_TQ = 512
DSA_TK = 256
ROW_BLOCK = 512
ROW_TOK = 256
NEG = -1e30
LOG2E = 1.4426950408889634
INT_MIN = -(2 ** 31)
VMEM_LIMIT = 56 * 1024 * 1024


def _mm(a, b):
    return jnp.dot(a.astype(BF16), b.astype(BF16), preferred_element_type=F32)


def _mm_nt(a, b):
    return lax.dot_general(a.astype(BF16), b.astype(BF16), (((1,), (1,)), ((), ())),
                           preferred_element_type=F32)


def _mm_tn(a, b):
    return lax.dot_general(a.astype(BF16), b.astype(BF16), (((0,), (0,)), ((), ())),
                           preferred_element_type=F32)


def _split3(x):
    hi = x.astype(BF16)
    r1 = x - hi.astype(F32)
    mid = r1.astype(BF16)
    lo = (r1 - mid.astype(F32)).astype(BF16)
    return hi, mid, lo


def _mm_hilo_rhs(x, b):
    hi = x.astype(BF16)
    lo = (x - hi.astype(F32)).astype(BF16)
    return jnp.dot(hi, b, preferred_element_type=F32) + jnp.dot(lo, b, preferred_element_type=F32)


def _mm_exact_lhs(a, x):
    hi, mid, lo = _split3(x)
    return (jnp.dot(a, hi, preferred_element_type=F32) + jnp.dot(a, mid, preferred_element_type=F32)
            + jnp.dot(a, lo, preferred_element_type=F32))


def _rms(x, g):
    return x * lax.rsqrt(jnp.mean(x * x, axis=-1, keepdims=True) + RMS_EPS) * g


def _sigmoid(x):
    return 1.0 / (1.0 + jnp.exp(-x))


def _softplus(x):
    return jnp.maximum(x, 0.0) + jnp.log1p(jnp.exp(-jnp.abs(x)))


def _load_row_tiles(ref, rows, sub):
    return jnp.concatenate([ref[pl.ds(c, rows, stride=sub), :] for c in range(sub)], axis=-1)


def _store_row_tiles(ref, val):
    rows, d = val.shape
    sub = d // LANE
    for c in range(sub):
        ref[pl.ds(c, rows, stride=sub), :] = val[:, c * LANE:(c + 1) * LANE]


def _cparams(sem):
    return pltpu.CompilerParams(dimension_semantics=sem, vmem_limit_bytes=VMEM_LIMIT)


def _const_spec(shape):
    nd = len(shape)
    return pl.BlockSpec(shape, lambda *_: (0,) * nd, pipeline_mode=pl.Buffered(1))


def _inproj_kernel(x_ref, g_ref, wrw_ref, wdsa_ref, wgate_ref, bgate_ref, wuk_ref, kvn_ref,
                   zrw_ref, qlat_ref, ckv_ref, qidx_ref, kidx_ref, widx_ref, gate_ref):
    hn = _rms(x_ref[...], g_ref[...]).astype(BF16)
    zrw_ref[...] = jnp.dot(hn, wrw_ref[...], preferred_element_type=F32)
    zd = jnp.dot(hn, wdsa_ref[...], preferred_element_type=F32)
    o = 0
    q = zd[:, o:o + DSA_WIDTH]; o += DSA_WIDTH
    c = zd[:, o:o + DSA_KV_RANK]; o += DSA_KV_RANK
    qi = zd[:, o:o + IDX_HEADS * LANE]; o += IDX_HEADS * LANE
    ki = zd[:, o:o + LANE]; o += LANE
    wi = zd[:, o:o + LANE]
    qlat_ref[...] = (_mm(q, wuk_ref[...]) * (DSA_HEAD_DIM ** -0.5 * LOG2E)).astype(BF16)
    cn = _rms(c, kvn_ref[...]).astype(BF16)
    ckv_ref[...] = jnp.concatenate([cn, jnp.ones_like(cn)], axis=-1)
    qidx_ref[...] = qi.astype(BF16)
    kidx_ref[...] = ki.astype(BF16)
    widx_ref[...] = wi * ((IDX_HEADS * IDX_DIM) ** -0.5)
    zg = jnp.dot(hn, wgate_ref[...], preferred_element_type=F32) + bgate_ref[...]
    gate_ref[...] = _sigmoid(zg)


def _inproj(x2, norm_mix, w_rw, w_dsa, w_gate, b_gate, wuk_bd, kv_norm, tm):
    T, D = x2.shape
    grid = (T // tm,)
    row = lambda n: pl.BlockSpec((tm, n), lambda i: (i, 0))
    outs = [
        jax.ShapeDtypeStruct((T, RW_ZCOLS), F32),
        jax.ShapeDtypeStruct((T, DSA_HEADS * DSA_KV_RANK), BF16),
        jax.ShapeDtypeStruct((T, 2 * DSA_KV_RANK), BF16),
        jax.ShapeDtypeStruct((T, IDX_HEADS * LANE), BF16),
        jax.ShapeDtypeStruct((T, LANE), BF16),
        jax.ShapeDtypeStruct((T, LANE), F32),
        jax.ShapeDtypeStruct((T, 2 * D), F32),
    ]
    return pl.pallas_call(
        _inproj_kernel,
        grid=grid,
        in_specs=[row(D), _const_spec((1, D)), _const_spec(w_rw.shape), _const_spec(w_dsa.shape),
                  _const_spec(w_gate.shape), _const_spec((1, 2 * D)), _const_spec(wuk_bd.shape),
                  _const_spec((1, DSA_KV_RANK))],
        out_specs=[row(s.shape[1]) for s in outs],
        out_shape=outs,
        compiler_params=_cparams(("parallel",)),
        name="inproj",
    )(x2, norm_mix.reshape(1, D), w_rw, w_dsa, w_gate, b_gate.reshape(1, 2 * D), wuk_bd,
      kv_norm.reshape(1, DSA_KV_RANK))


def _rwkv_kernel(z_ref, mu_ref, w0_ref, w2_ref, a0_ref, a2_ref, g2_ref, kk_ref, ka_ref, rk_ref,
                 lnw_ref, lnb_ref, hs_ref, y_ref, state_ref, carry_ref, ybuf_ref):
    C, N, H = RW_CHUNK, RW_HEAD_DIM, RW_HEADS
    W = RW_WIDTH
    NB = z_ref.shape[0]

    @pl.when(pl.program_id(1) == 0)
    def _():
        state_ref[...] = jnp.zeros_like(state_ref)
        carry_ref[...] = jnp.zeros_like(carry_ref)

    hs = hs_ref[...]
    headsum = lambda t: _mm_hilo_rhs(t, hs)
    ti = lax.broadcasted_iota(I32, (C, C), 0)
    si = lax.broadcasted_iota(I32, (C, C), 1)
    ltri = (si <= ti).astype(BF16)
    t2 = lax.broadcasted_iota(I32, (C, 2 * C), 0)
    s2 = lax.broadcasted_iota(I32, (C, 2 * C), 1) % C
    m_strict = s2 < t2
    m_incl = s2 <= t2
    eye = (si == ti).astype(F32)
    zeros_cn = jnp.zeros((C, N), F32)
    row = lax.broadcasted_iota(I32, (C, RW_ZCOLS), 0)

    a_t, r_t, b_t, k_t, v, p_c, bonus, g = ([None] * NB for _ in range(8))
    for bi in range(NB):
        z = z_ref[bi]
        zprev = jnp.where(row == 0, carry_ref[bi], pltpu.roll(z, 1, 0))
        carry_ref[bi] = z[C - 1:C, :]
        zm = z + (zprev - z) * mu_ref[...]
        r = zm[:, 0:W]
        k = zm[:, W:2 * W]
        v[bi] = zm[:, 2 * W:3 * W]
        xw = zm[:, 3 * W:3 * W + LANE]
        xa = zm[:, 3 * W + LANE:3 * W + 2 * LANE]
        xg = zm[:, 3 * W + 2 * LANE:3 * W + 3 * LANE]
        wlog = -_softplus(-(w0_ref[...] + _mm(jnp.tanh(xw), w2_ref[...]))) - 0.5
        dlog = -jnp.exp(wlog)
        a = _sigmoid(a0_ref[...] + _mm(xa, a2_ref[...]))
        g[bi] = _mm(_sigmoid(xg), g2_ref[...])
        kk = k * kk_ref[...]
        kk = kk / jnp.maximum(jnp.sqrt(headsum(kk * kk)), 1e-12)
        k_eff = k * (1.0 + (a - 1.0) * ka_ref[...])
        bonus[bi] = headsum(r * k_eff * rk_ref[...]) * v[bi]
        cum = _mm_exact_lhs(ltri, dlog)
        pin = jnp.exp(cum)
        pinv = jnp.exp(-cum)
        a_t[bi] = -kk * jnp.exp(cum - dlog)
        r_t[bi] = r * pin
        b_t[bi] = kk * a * pinv
        k_t[bi] = k_eff * pinv
        p_c[bi] = pin[C - 1:C, :]

    units = [(bi, h) for bi in range(NB) for h in range(H)]
    ur = range(len(units))
    sl = lambda h: slice(h * N, (h + 1) * N)
    ar = [jnp.concatenate([a_t[bi][:, sl(h)], r_t[bi][:, sl(h)]], axis=0).astype(BF16) for bi, h in units]
    bk = [jnp.concatenate([b_t[bi][:, sl(h)], k_t[bi][:, sl(h)]], axis=0).astype(BF16) for bi, h in units]
    vh = [v[bi][:, sl(h)] for bi, h in units]
    s0 = [state_ref[bi, h] for bi, h in units]
    q = [_mm_nt(ar[u], bk[u]) for u in ur]
    w_all = [_mm_nt(ar[u], s0[u]) for u in ur]
    top = [jnp.where(m_strict, q[u][:C], 0.0).astype(BF16) for u in ur]
    bot = [jnp.where(m_incl, q[u][C:], 0.0).astype(BF16) for u in ur]
    rhs = [w_all[u][:C] + _mm(top[u], jnp.concatenate([zeros_cn, vh[u]], axis=0)) for u in ur]
    pw = [top[u][:, :C] for u in ur]
    inv = [eye + pw[u].astype(F32) for u in ur]
    for _ in range(5):
        pw = [_mm(pw[u], pw[u]).astype(BF16) for u in ur]
        inv = [inv[u] + _mm(pw[u], inv[u]) for u in ur]
    uu = [_mm(inv[u], rhs[u]) for u in ur]
    gsrc = [jnp.concatenate([uu[u], vh[u]], axis=0).astype(BF16) for u in ur]
    for u, (bi, h) in enumerate(units):
        ybuf_ref[bi, :, sl(h)] = w_all[u][C:] + _mm(bot[u], gsrc[u])
    for u, (bi, h) in enumerate(units):
        state_ref[bi, h] = (s0[u] + _mm_tn(gsrc[u], bk[u])) * p_c[bi][:, sl(h)]

    for bi in range(NB):
        y = ybuf_ref[bi]
        mean = headsum(y) * (1.0 / N)
        d = y - mean
        var = headsum(d * d) * (1.0 / N)
        yn = d * lax.rsqrt(var + RW_GN_EPS) * lnw_ref[...] + lnb_ref[...]
        y_ref[bi] = ((yn + bonus[bi]) * g[bi]).astype(y_ref.dtype)


def _rwkv(z_rw, B, S, mu, w0, w2, a0, a2, g2, k_k, k_a, r_k, ln_w, ln_b):
    C, W = RW_CHUNK, RW_WIDTH
    nb = RW_SEQS if B % RW_SEQS == 0 else 1
    nc = S // C
    hid = jnp.arange(W) // RW_HEAD_DIM
    hs = (hid[:, None] == hid[None, :]).astype(BF16)
    vec = lambda t: t.reshape(1, -1)
    consts = [vec(mu), vec(w0), w2, vec(a0), a2, g2, vec(k_k), vec(k_a), vec(r_k), vec(ln_w),
              vec(ln_b), hs]
    y = pl.pallas_call(
        _rwkv_kernel,
        grid=(B // nb, nc),
        in_specs=[pl.BlockSpec((nb, C, RW_ZCOLS), lambda b, c: (b, c, 0))]
                 + [_const_spec(t.shape) for t in consts],
        out_specs=pl.BlockSpec((nb, C, W), lambda b, c: (b, c, 0)),
        out_shape=jax.ShapeDtypeStruct((B, S, W), BF16),
        scratch_shapes=[pltpu.VMEM((nb, RW_HEADS, RW_HEAD_DIM, RW_HEAD_DIM), F32),
                        pltpu.VMEM((nb, 1, RW_ZCOLS), F32),
                        pltpu.VMEM((nb, C, W), F32)],
        compiler_params=_cparams(("parallel", "arbitrary")),
        name="rwkv",
    )(z_rw.reshape(B, S, RW_ZCOLS), *consts)
    return y.reshape(B * S, W)


def _dsa_kernel(qlat_ref, qidx_ref, widx_ref, ckv_ref, ckvt_ref, kidx_ref, wuvt_ref, y_ref,
                  keys_ref, hi_ref, lo_ref, bias_ref, m_ref, acc_ref, *, k_sel):
    TQ, TK = DSA_TQ, DSA_TK
    HQ = 256
    SUBP = 16
    I16 = jnp.int16
    HALF = 1 << 15
    R = DSA_KV_RANK
    qi = pl.program_id(1)
    nkt = (qi + 1) * (TQ // TK)
    npair = nkt // 2
    keypos = lax.broadcasted_iota(I32, (TK, TQ), 0)
    rowpos = qi * TQ + lax.broadcasted_iota(I32, (TK, TQ), 1)
    one16 = jnp.ones((), BF16)
    zero16 = jnp.zeros((), BF16)

    def score_body(kt, carry):
        kx = kidx_ref[pl.ds(pl.multiple_of(kt * TK, TK), TK), :]
        sc = jnp.zeros((TK, TQ), F32)
        for h in range(IDX_HEADS):
            rel = jnp.maximum(_mm_nt(kx, qidx_ref[:, h * LANE:(h + 1) * LANE]), 0.0)
            sc = sc + widx_ref[h:h + 1, :] * rel
        sc = jnp.where(sc == 0.0, 0.0, sc)
        bits = pltpu.bitcast(sc, I32)
        key = jnp.where(bits >= 0, bits, bits ^ jnp.int32(0x7FFFFFFF))
        key = jnp.where(kt * TK + keypos <= rowpos, key, jnp.int32(INT_MIN))
        keys_ref[kt] = key
        hi_ref[kt] = (key >> 16).astype(I16)
        return carry

    lax.fori_loop(0, nkt, score_body, 0)

    def count16(ref, level, strict):
        lv = jnp.broadcast_to(level, (SUBP, TQ)).astype(I16)

        def body(kp, acc):
            for j in range(2):
                for c in range(TK // SUBP):
                    kk = ref[2 * kp + j, c * SUBP:(c + 1) * SUBP, :]
                    acc = acc + jnp.where((kk > lv) if strict else (kk >= lv), one16, zero16)
            return acc

        acc = lax.fori_loop(0, npair, body, jnp.zeros((SUBP, TQ), BF16))
        return jnp.sum(acc.astype(F32), axis=0, keepdims=True)

    def select16(ref, k_row):
        def bit_body(i, tu):
            cand = tu | lax.shift_left(jnp.int32(1), 15 - i)
            return jnp.where(count16(ref, cand - HALF, strict=False) >= k_row, cand, tu)
        return lax.fori_loop(0, 16, bit_body, jnp.zeros((1, TQ), I32))

    th = select16(hi_ref, jnp.full((1, TQ), k_sel, F32)) - HALF
    n_gt_hi = count16(hi_ref, th, strict=True)

    def lo_body(kt, carry):
        key = keys_ref[kt]
        lo = (key & jnp.int32(0xFFFF)) - HALF
        lo_ref[kt] = jnp.where((key >> 16) == th, lo, -HALF).astype(I16)
        return carry

    lax.fori_loop(0, nkt, lo_body, 0)
    tl = select16(lo_ref, k_sel - n_gt_hi)
    n_gt = n_gt_hi + count16(lo_ref, tl - HALF, strict=True)
    thr = th * (1 << 16) + tl
    need = k_sel - n_gt

    m_ref[...] = jnp.full_like(m_ref, NEG)
    acc_ref[...] = jnp.zeros_like(acc_ref)
    lrow = lax.broadcasted_iota(I32, (TK + 8, TK), 0)
    lext = ((lax.broadcasted_iota(I32, (TK + 8, TK), 1) < lrow) | (lrow >= TK)).astype(BF16)

    def att_body(kp, eqc):
        for j in range(2):
            kt = 2 * kp + j
            key = keys_ref[kt]
            eq = key == thr
            pre = jnp.dot(lext, eq.astype(F32).astype(BF16), preferred_element_type=F32)
            sel = ((key > thr) | (eq & (pre[:TK] + eqc < need))) & (kt * TK + keypos <= rowpos)
            bias_ref[j * TK:(j + 1) * TK, :] = jnp.where(sel, 0.0, NEG)
            eqc = eqc + pre[TK:TK + 1]
        ck = ckv_ref[pl.ds(pl.multiple_of(kp * (2 * TK), 2 * TK), 2 * TK), :]
        ctt = ckvt_ref[kp]
        for h in range(DSA_HEADS):
            for r0 in range(0, TQ, HQ):
                s = _mm_nt(ck, qlat_ref[r0:r0 + HQ, h * R:(h + 1) * R]) + bias_ref[:, r0:r0 + HQ]
                m_old = m_ref[h, :, r0:r0 + HQ]
                m_new = jnp.maximum(m_old, jnp.max(s, axis=0, keepdims=True))
                p = jnp.exp2(s - m_new[0:1])
                acc_ref[h, :, r0:r0 + HQ] = (jnp.exp2(m_old - m_new)[0:1] * acc_ref[h, :, r0:r0 + HQ]
                                             + jnp.dot(ctt, p.astype(BF16), preferred_element_type=F32))
                m_ref[h, :, r0:r0 + HQ] = m_new
        return eqc

    lax.fori_loop(0, npair, att_body, jnp.zeros((1, TQ), F32))
    o = jnp.concatenate([(acc_ref[h, :R, :] / acc_ref[h, R:, :]).astype(BF16)
                         for h in range(DSA_HEADS)], axis=0)
    y_ref[...] = jnp.dot(wuvt_ref[...], o, preferred_element_type=F32).astype(y_ref.dtype)


def _dsa(qlat, qidx, widx, ckv, kidx, wuv_bd, B, S):
    TQ, TK = DSA_TQ, DSA_TK
    R = DSA_KV_RANK
    nq = S // TQ
    nk = S // TK
    assert TQ == 2 * TK and S % TQ == 0 and nk * TK // 16 <= 256
    k_sel = min(IDX_TOPK_MAX, S // 4)
    widx_t = widx[:, :8].T
    ckv_t = ckv.reshape(B * nq, 2 * TK, 2 * R).transpose(0, 2, 1)
    qrow = lambda n: pl.BlockSpec((TQ, n), lambda b, q: (b * nq + q, 0))
    y_t = pl.pallas_call(
        functools.partial(_dsa_kernel, k_sel=k_sel),
        grid=(B, nq),
        in_specs=[qrow(DSA_HEADS * R), qrow(IDX_HEADS * LANE),
                  pl.BlockSpec((8, TQ), lambda b, q: (0, b * nq + q)),
                  pl.BlockSpec((S, R), lambda b, q: (b, 0)),
                  pl.BlockSpec((nq, 2 * R, 2 * TK), lambda b, q: (b, 0, 0)),
                  pl.BlockSpec((S, LANE), lambda b, q: (b, 0)),
                  _const_spec((DSA_WIDTH, DSA_HEADS * R))],
        out_specs=pl.BlockSpec((None, DSA_WIDTH, TQ), lambda b, q: (b, 0, q)),
        out_shape=jax.ShapeDtypeStruct((B, DSA_WIDTH, S), BF16),
        scratch_shapes=[pltpu.VMEM((nk, TK, TQ), I32), pltpu.VMEM((nk, TK, TQ), jnp.int16),
                        pltpu.VMEM((nk, TK, TQ), jnp.int16),
                        pltpu.VMEM((2 * TK, TQ), F32),
                        pltpu.VMEM((DSA_HEADS, 8, TQ), F32),
                        pltpu.VMEM((DSA_HEADS, 2 * R, TQ), F32)],
        compiler_params=_cparams(("parallel", "arbitrary")),
        name="dsa",
    )(qlat, qidx, widx_t, ckv, ckv_t, kidx, wuv_bd.T)
    return y_t.transpose(0, 2, 1).reshape(B * S, DSA_WIDTH)


def _memkv_kernel(m_ref, g_ref, w_ref, o_ref):
    o_ref[...] = _mm(_rms(m_ref[...], g_ref[...]), w_ref[...]).astype(o_ref.dtype)


def _memkv(mem2, norm_mem, w_ckv, M):
    R, D = mem2.shape
    return pl.pallas_call(
        _memkv_kernel,
        grid=(R // M,),
        in_specs=[pl.BlockSpec((M, D), lambda i: (i, 0)), _const_spec((1, D)), _const_spec(w_ckv.shape)],
        out_specs=pl.BlockSpec((M, w_ckv.shape[1]), lambda i: (i, 0)),
        out_shape=jax.ShapeDtypeStruct((R, w_ckv.shape[1]), BF16),
        compiler_params=_cparams(("parallel",)),
        name="memkv",
    )(mem2, norm_mem.reshape(1, D), w_ckv)


def _mix_kernel(x_ref, ya_ref, yb_ref, gate_ref, kv_ref, wpa_ref, wpb_ref, wout_ref, ncross_ref,
                wcq_ref, wco_ref, nffn_ref, wrh_ref, wrl_ref, br_ref,
                h2_ref, hn3_ref, ids_ref, gts_ref):
    D = x_ref.shape[1]
    xw = D // X_HEADS
    a = jnp.dot(ya_ref[...], wpa_ref[...], preferred_element_type=F32)
    b = jnp.dot(yb_ref[...], wpb_ref[...], preferred_element_type=F32)
    mix = gate_ref[:, :D] * a + gate_ref[:, D:] * b
    h1 = x_ref[...] + _mm(mix, wout_ref[...])
    q = _mm(_rms(h1, ncross_ref[...]), wcq_ref[...]).astype(BF16)
    outs = []
    for h in range(X_HEADS):
        s = _mm_nt(q[:, h * xw:(h + 1) * xw], kv_ref[:, h * xw:(h + 1) * xw]) * (xw ** -0.5)
        e = jnp.exp(s - jnp.max(s, axis=-1, keepdims=True))
        p = e / jnp.sum(e, axis=-1, keepdims=True)
        outs.append(_mm(p, kv_ref[:, D + h * xw:D + (h + 1) * xw]).astype(BF16))
    h2 = h1 + jnp.dot(jnp.concatenate(outs, axis=-1), wco_ref[...], preferred_element_type=F32)
    h2_ref[...] = h2
    hn3 = _rms(h2, nffn_ref[...])
    _store_row_tiles(hn3_ref, hn3)

    hi = hn3.astype(BF16)
    lo = (hn3 - hi.astype(F32)).astype(BF16)
    lg = (jnp.dot(hi, wrh_ref[...], preferred_element_type=F32)
          + jnp.dot(lo, wrh_ref[...], preferred_element_type=F32)
          + jnp.dot(hi, wrl_ref[...], preferred_element_type=F32)) + br_ref[...]
    lane = lax.broadcasted_iota(I32, lg.shape, 1)
    big = jnp.int32(1 << 20)
    rmax = lambda t: jnp.max(t, axis=-1, keepdims=True)
    rsum = lambda t: jnp.sum(t, axis=-1, keepdims=True)
    first = lambda hit: jnp.min(jnp.where(hit, lane, big), axis=-1, keepdims=True)
    gmask = lane < N_GROUPS
    gl = jnp.where(gmask, lg, NEG)
    ge = jnp.where(gmask, jnp.exp(gl - rmax(gl)), 0.0)
    gp = jnp.where(gmask, ge / rsum(ge), -1.0)
    p_grp = rmax(gp)
    grp = first(gp == p_grp)
    eidx = lane - N_GROUPS
    emask = (eidx >= 0) & (eidx < N_EXPERTS) & ((eidx // EXPERTS_PER_GROUP) == grp)
    el = jnp.where(emask, lg, NEG)
    ee = jnp.where(emask, jnp.exp(el - rmax(el)), 0.0)
    ep = jnp.where(emask, ee / rsum(ee), -1.0)
    p1 = rmax(ep)
    i1 = first(ep == p1)
    ep2 = jnp.where(lane == i1, -1.0, ep)
    p2 = rmax(ep2)
    i2 = first(ep2 == p2)
    den = p1 + p2
    ids_ref[...] = jnp.where(lane == 0, i1 - N_GROUPS, jnp.where(lane == 1, i2 - N_GROUPS, 0))
    gts_ref[...] = jnp.where(lane == 0, p_grp * p1 / den, jnp.where(lane == 1, p_grp * p2 / den, 0.0))


def _mix(x2, ya, yb, gates, kv, wpa, wpb, wout, ncross, wcq, wco, nffn, wr_hi, wr_lo, br, B, S, M, tm):
    T, D = x2.shape
    nt = S // tm
    row = lambda n: pl.BlockSpec((tm, n), lambda i: (i, 0))
    consts = [wpa, wpb, wout, ncross.reshape(1, D), wcq, wco, nffn.reshape(1, D), wr_hi, wr_lo, br]
    outs = [jax.ShapeDtypeStruct((T, D), F32), jax.ShapeDtypeStruct((T * (D // LANE), LANE), F32),
            jax.ShapeDtypeStruct((T, LANE), I32), jax.ShapeDtypeStruct((T, LANE), F32)]
    return pl.pallas_call(
        _mix_kernel,
        grid=(T // tm,),
        in_specs=[row(D), row(RW_WIDTH), row(DSA_WIDTH), row(2 * D),
                  pl.BlockSpec((M, 2 * D), lambda i: (i // nt, 0))]
                 + [_const_spec(t.shape) for t in consts],
        out_specs=[row(D), pl.BlockSpec((tm * (D // LANE), LANE), lambda i: (i, 0)), row(LANE), row(LANE)],
        out_shape=outs,
        compiler_params=_cparams(("parallel",)),
        name="mix_cross_router",
    )(x2, ya, yb, gates, kv, *consts)


def _dispatch_kernel(slot_ref, src_ref, _, out_ref, sem, *, sub):
    n_tok = src_ref.shape[0] // sub
    tile = lambda ref, r: ref.at[pl.ds(pl.multiple_of(r * sub, sub), sub), :]

    def start(t, c):
        for j in range(2):
            pltpu.make_async_copy(tile(src_ref, t), tile(out_ref, slot_ref[0, 0, 2 * t + j]), sem).start()
        return c

    lax.fori_loop(0, n_tok, start, 0)
    for j in range(2):
        pltpu.make_async_copy(src_ref, out_ref.at[pl.ds(0, n_tok * sub), :], sem).wait()


def _dispatch(slots, src, out_rows, sub):
    T = src.shape[0] // sub
    td = ROW_TOK
    anyspec = pl.BlockSpec(memory_space=pl.ANY)
    return pl.pallas_call(
        functools.partial(_dispatch_kernel, sub=sub),
        grid=(T // td,),
        in_specs=[pl.BlockSpec((1, 1, 2 * td), lambda i: (i, 0, 0), memory_space=pltpu.SMEM),
                  pl.BlockSpec((td * sub, LANE), lambda i: (i, 0)), anyspec],
        out_specs=anyspec,
        out_shape=jax.ShapeDtypeStruct((out_rows * sub, LANE), src.dtype),
        scratch_shapes=[pltpu.SemaphoreType.DMA(())],
        input_output_aliases={2: 0},
        compiler_params=_cparams(("arbitrary",)),
        name="moe_dispatch",
    )(slots.reshape(T // td, 1, 2 * td), src, jnp.zeros((out_rows * sub, LANE), src.dtype))


def _expert_kernel(be_ref, nu_ref, x_ref, wg_ref, wu_ref, wd_ref, o_ref):
    i = pl.program_id(0)

    @pl.when(i < nu_ref[0])
    def _():
        sub = wg_ref.shape[0] // LANE
        x = _load_row_tiles(x_ref, MOE_BLOCK, sub).astype(BF16)
        gt = jnp.dot(x, wg_ref[...], preferred_element_type=F32)
        up = jnp.dot(x, wu_ref[...], preferred_element_type=F32)
        hid = gt * _sigmoid(gt) * up
        _store_row_tiles(o_ref, _mm(hid, wd_ref[...]))

    @pl.when(i >= nu_ref[0])
    def _():
        o_ref[...] = jnp.zeros_like(o_ref)


def _experts(xs, blk_expert, n_used, w_gate, w_up, w_down):
    E, D, F = w_gate.shape
    sub = D // LANE
    P = xs.shape[0] // sub
    nb = P // MOE_BLOCK
    grid_spec = pltpu.PrefetchScalarGridSpec(
        num_scalar_prefetch=2,
        grid=(nb,),
        in_specs=[pl.BlockSpec((MOE_BLOCK * sub, LANE), lambda i, be, nu: (i, 0)),
                  pl.BlockSpec((None, D, F), lambda i, be, nu: (be[i], 0, 0)),
                  pl.BlockSpec((None, D, F), lambda i, be, nu: (be[i], 0, 0)),
                  pl.BlockSpec((None, F, D), lambda i, be, nu: (be[i], 0, 0))],
        out_specs=pl.BlockSpec((MOE_BLOCK * sub, LANE), lambda i, be, nu: (i, 0)),
    )
    return pl.pallas_call(
        _expert_kernel,
        grid_spec=grid_spec,
        out_shape=jax.ShapeDtypeStruct((P * sub, LANE), F32),
        compiler_params=_cparams(("arbitrary",)),
        name="experts",
    )(blk_expert, n_used, xs, w_gate, w_up, w_down)


def _final_kernel(slot_ref, nslot_ref, h_ref, gts_ref, g_ref, ys_ref, o_ref, buf_ref, sem, *, apply_norm):
    n_tok, d = h_ref.shape
    sub = d // LANE
    tile = lambda ref, r: ref.at[pl.ds(pl.multiple_of(r * sub, sub), sub), :]
    i = pl.program_id(0)
    cur = i % 2

    def gather(s_ref, b):
        def start(t, c):
            for j in range(2):
                pltpu.make_async_copy(tile(ys_ref, s_ref[0, 0, 2 * t + j]), tile(buf_ref.at[b, j], t),
                                      sem.at[b]).start()
            return c
        lax.fori_loop(0, n_tok, start, 0)

    @pl.when(i == 0)
    def _():
        gather(slot_ref, 0)

    @pl.when(i + 1 < pl.num_programs(0))
    def _():
        gather(nslot_ref, 1 - cur)

    for j in range(2):
        pltpu.make_async_copy(ys_ref.at[pl.ds(0, n_tok * sub), :], buf_ref.at[cur, j], sem.at[cur]).wait()
    y = (gts_ref[:, 0:1] * _load_row_tiles(buf_ref.at[cur, 0], n_tok, sub)
         + gts_ref[:, 1:2] * _load_row_tiles(buf_ref.at[cur, 1], n_tok, sub))
    h = h_ref[...] + y
    o_ref[...] = _rms(h, g_ref[...]) if apply_norm else h


def _final(h2, ys, slots, gts, norm_final, apply_norm):
    T, D = h2.shape
    tm = ROW_TOK
    nt = T // tm
    row = lambda n: pl.BlockSpec((tm, n), lambda i: (i, 0))
    slots3 = slots.reshape(nt, 1, 2 * tm)
    return pl.pallas_call(
        functools.partial(_final_kernel, apply_norm=apply_norm),
        grid=(nt,),
        in_specs=[pl.BlockSpec((1, 1, 2 * tm), lambda i: (i, 0, 0), memory_space=pltpu.SMEM),
                  pl.BlockSpec((1, 1, 2 * tm), lambda i: (jnp.minimum(i + 1, nt - 1), 0, 0),
                               memory_space=pltpu.SMEM),
                  row(D), row(LANE), _const_spec((1, D)), pl.BlockSpec(memory_space=pl.ANY)],
        out_specs=row(D),
        out_shape=jax.ShapeDtypeStruct((T, D), F32),
        scratch_shapes=[pltpu.VMEM((2, 2, tm * (D // LANE), LANE), F32), pltpu.SemaphoreType.DMA((2,))],
        compiler_params=_cparams(("arbitrary",)),
        name="final",
    )(slots3, slots3, h2, gts, norm_final.reshape(1, D), ys)


def _pad_cols(w, n):
    return jnp.pad(w, ((0, 0), (0, n - w.shape[1])))


def _pad_rows(w, n):
    return jnp.pad(w, ((0, n - w.shape[0]), (0, 0)))


def _layer(h, mem, norm_mix, w_in, shift_mu, rw_w0, rw_w2, rw_a0, rw_a2, rw_g2, rw_k_k, rw_k_a,
           rw_r_k, rw_ln_w, rw_ln_b, kv_norm, w_uk, w_uv, w_proj_a, w_proj_b, b_gate, w_out,
           norm_cross, norm_mem, w_cq, w_ckv, w_co, norm_ffn, w_router_g, b_router_g, w_router_e,
           b_router_e, w_e_gate, w_e_up, w_e_down):
    B, S, D = h.shape
    M = mem.shape[1]
    T = B * S
    W = RW_WIDTH
    x2 = h.reshape(T, D)

    c = 0
    pieces = {}
    for name, n in (("r", W), ("k", W), ("v", W), ("xw", RW_DECAY_LORA), ("xa", RW_AAA_LORA),
                    ("xg", RW_GATE_LORA), ("q", DSA_WIDTH), ("ckv", DSA_KV_RANK),
                    ("qidx", IDX_HEADS * IDX_DIM), ("kidx", IDX_DIM), ("widx", IDX_HEADS),
                    ("gate", 2 * D)):
        pieces[name] = (c, n)
        c += n
    col = lambda name: w_in[:, pieces[name][0]:pieces[name][0] + pieces[name][1]]
    mu_of = lambda name: shift_mu[pieces[name][0]:pieces[name][0] + pieces[name][1]]
    w_rw = jnp.concatenate([col("r"), col("k"), col("v"), _pad_cols(col("xw"), LANE),
                            _pad_cols(col("xa"), LANE), _pad_cols(col("xg"), LANE)], axis=1).astype(BF16)
    padv = lambda t: jnp.pad(t, (0, LANE - t.shape[0]))
    mu = jnp.concatenate([mu_of("r"), mu_of("k"), mu_of("v"), padv(mu_of("xw")), padv(mu_of("xa")),
                          padv(mu_of("xg"))])
    qidx_w = col("qidx").reshape(D, IDX_HEADS, IDX_DIM)
    qidx_w = jnp.pad(qidx_w, ((0, 0), (0, 0), (0, LANE - IDX_DIM))).reshape(D, IDX_HEADS * LANE)
    w_dsa = jnp.concatenate([col("q"), col("ckv"), qidx_w, _pad_cols(col("kidx"), LANE),
                             _pad_cols(col("widx"), LANE)], axis=1).astype(BF16)
    w_gate_in = col("gate").astype(BF16)
    eye_h = jnp.eye(DSA_HEADS, dtype=F32)
    wuk_bd = jnp.einsum("rhd,hg->hdgr", w_uk, eye_h).reshape(DSA_WIDTH, DSA_HEADS * DSA_KV_RANK).astype(BF16)
    wuv_bd = jnp.einsum("rhd,hg->hrgd", w_uv, eye_h).reshape(DSA_HEADS * DSA_KV_RANK, DSA_WIDTH).astype(BF16)

    tm1 = min(ROW_BLOCK, T)
    z_rw, qlat, ckv, qidx, kidx, widx, gates = _inproj(
        x2, norm_mix, w_rw, w_dsa, w_gate_in, b_gate, wuk_bd, kv_norm, tm1)

    y_a = _rwkv(z_rw, B, S, mu, rw_w0, _pad_rows(rw_w2, LANE).astype(BF16), rw_a0,
                _pad_rows(rw_a2, LANE).astype(BF16), rw_g2.astype(BF16), rw_k_k, rw_k_a, rw_r_k,
                rw_ln_w, rw_ln_b)
    y_b = _dsa(qlat, qidx, widx, ckv, kidx, wuv_bd, B, S)
    kv = _memkv(mem.reshape(B * M, D), norm_mem, w_ckv.astype(BF16), M)

    w_r = jnp.concatenate([w_router_g, w_router_e], axis=1)
    w_r = _pad_cols(w_r, LANE)
    wr_hi = w_r.astype(BF16)
    wr_lo = (w_r - wr_hi.astype(F32)).astype(BF16)
    b_r = jnp.pad(jnp.concatenate([b_router_g, b_router_e]), (0, LANE - N_GROUPS - N_EXPERTS)).reshape(1, LANE)
    tm5 = min(ROW_BLOCK, S)
    h2, hn3, ids, gts = _mix(x2, y_a, y_b, gates, kv, w_proj_a.astype(BF16), w_proj_b.astype(BF16),
                             w_out.astype(BF16), norm_cross, w_cq.astype(BF16), w_co.astype(BF16),
                             norm_ffn, wr_hi, wr_lo, b_r, B, S, M, tm5)

    A = 2 * T
    e_flat = ids[:, :2].reshape(A)
    onehot = (e_flat[:, None] == jnp.arange(N_EXPERTS, dtype=I32)[None, :]).astype(I32)
    csum = jnp.cumsum(onehot, axis=0)
    rank = jnp.sum(csum * onehot, axis=1) - 1
    counts = csum[-1]
    padded = (counts + MOE_BLOCK - 1) // MOE_BLOCK * MOE_BLOCK
    pend = jnp.cumsum(padded)
    pstart = pend - padded
    slots = (jnp.sum(pstart[None, :] * onehot, axis=1) + rank).astype(I32)
    P = A + N_EXPERTS * MOE_BLOCK
    nb = P // MOE_BLOCK
    blk_pos = jnp.arange(nb, dtype=I32) * MOE_BLOCK
    blk_expert = jnp.minimum(jnp.sum((pend[None, :] <= blk_pos[:, None]).astype(I32), axis=1),
                             N_EXPERTS - 1).astype(I32)
    n_used = (pend[-1:] // MOE_BLOCK).astype(I32)

    xs = _dispatch(slots, hn3, P, D // LANE)
    ys = _experts(xs, blk_expert, n_used, w_e_gate.astype(BF16), w_e_up.astype(BF16),
                  w_e_down.astype(BF16))
    return h2, ys, slots, gts


def kernel(x, mem, norm_mix, w_in, shift_mu, rw_w0, rw_w2, rw_a0, rw_a2, rw_g2, rw_k_k, rw_k_a, rw_r_k, rw_ln_w, rw_ln_b, kv_norm, w_uk, w_uv, w_proj_a, w_proj_b, b_gate, w_out, norm_cross, norm_mem, w_cq, w_ckv, w_co, norm_ffn, w_router_g, b_router_g, w_router_e, b_router_e, w_e_gate, w_e_up, w_e_down, norm_final):
    B, S, D = x.shape
    depth = norm_mix.shape[0]
    h = x
    for l in range(depth):
        last = l == depth - 1
        h2, ys, slots, gts = _layer(
            h, mem, norm_mix[l], w_in[l], shift_mu[l], rw_w0[l], rw_w2[l], rw_a0[l], rw_a2[l],
            rw_g2[l], rw_k_k[l], rw_k_a[l], rw_r_k[l], rw_ln_w[l], rw_ln_b[l], kv_norm[l], w_uk[l],
            w_uv[l], w_proj_a[l], w_proj_b[l], b_gate[l], w_out[l], norm_cross[l], norm_mem[l],
            w_cq[l], w_ckv[l], w_co[l], norm_ffn[l], w_router_g[l], b_router_g[l], w_router_e[l],
            b_router_e[l], w_e_gate[l], w_e_up[l], w_e_down[l])
        h = _final(h2, ys, slots, gts, norm_final, last).reshape(B, S, D)
    return h
```

```python
import functools

import jax
import jax.numpy as jnp
from jax import lax
from jax.experimental import pallas as pl
from jax.experimental.pallas import tpu as pltpu

F32 = jnp.float32
BF16 = jnp.bfloat16
I32 = jnp.int32

RMS_EPS = 1e-6
RW_HEADS = 8
RW_HEAD_DIM = 64
RW_WIDTH = RW_HEADS * RW_HEAD_DIM
RW_DECAY_LORA = 64
RW_AAA_LORA = 64
RW_GATE_LORA = 128
RW_GN_EPS = 64e-5
DSA_HEADS = 8
DSA_HEAD_DIM = 64
DSA_WIDTH = DSA_HEADS * DSA_HEAD_DIM
DSA_KV_RANK = 128
IDX_HEADS = 4
IDX_DIM = 64
IDX_TOPK_MAX = 256
X_HEADS = 4
N_GROUPS = 4
EXPERTS_PER_GROUP = 8
N_EXPERTS = N_GROUPS * EXPERTS_PER_GROUP
MOE_BLOCK = 256

LANE = 128
RW_CHUNK = 64
RW_SEQS = 4
RW_ZCOLS = 3 * RW_WIDTH + 3 * LANE
DSA_ZCOLS = DSA_WIDTH + DSA_KV_RANK + IDX_HEADS * LANE + LANE + LANE
DSA_TQ = 512
DSA_TK = 256
DSA_ONES = 128
ROW_BLOCK = 512
ROW_TOK = 256
NEG = -1e30
LOG2E = 1.4426950408889634
INT_MIN = -(2 ** 31)
VMEM_LIMIT = 56 * 1024 * 1024


def _mm(a, b):
    return jnp.dot(a.astype(BF16), b.astype(BF16), preferred_element_type=F32)


def _mm_nt(a, b):
    return lax.dot_general(a.astype(BF16), b.astype(BF16), (((1,), (1,)), ((), ())),
                           preferred_element_type=F32)


def _mm_tn(a, b):
    return lax.dot_general(a.astype(BF16), b.astype(BF16), (((0,), (0,)), ((), ())),
                           preferred_element_type=F32)


def _split3(x):
    hi = x.astype(BF16)
    r1 = x - hi.astype(F32)
    mid = r1.astype(BF16)
    lo = (r1 - mid.astype(F32)).astype(BF16)
    return hi, mid, lo


def _mm_hilo_rhs(x, b):
    hi = x.astype(BF16)
    lo = (x - hi.astype(F32)).astype(BF16)
    return jnp.dot(hi, b, preferred_element_type=F32) + jnp.dot(lo, b, preferred_element_type=F32)


def _mm_exact_lhs(a, x):
    hi, mid, lo = _split3(x)
    return (jnp.dot(a, hi, preferred_element_type=F32) + jnp.dot(a, mid, preferred_element_type=F32)
            + jnp.dot(a, lo, preferred_element_type=F32))


def _rms(x, g):
    return x * lax.rsqrt(jnp.mean(x * x, axis=-1, keepdims=True) + RMS_EPS) * g


def _sigmoid(x):
    return 1.0 / (1.0 + jnp.exp(-x))


def _softplus(x):
    return jnp.maximum(x, 0.0) + jnp.log1p(jnp.exp(-jnp.abs(x)))


def _load_row_tiles(ref, rows, sub):
    return jnp.concatenate([ref[pl.ds(c, rows, stride=sub), :] for c in range(sub)], axis=-1)


def _store_row_tiles(ref, val):
    rows, d = val.shape
    sub = d // LANE
    for c in range(sub):
        ref[pl.ds(c, rows, stride=sub), :] = val[:, c * LANE:(c + 1) * LANE]


def _cparams(sem):
    return pltpu.CompilerParams(dimension_semantics=sem, vmem_limit_bytes=VMEM_LIMIT)


def _const_spec(shape):
    nd = len(shape)
    return pl.BlockSpec(shape, lambda *_: (0,) * nd, pipeline_mode=pl.Buffered(1))


def _inproj_kernel(x_ref, g_ref, wrw_ref, wdsa_ref, wgate_ref, bgate_ref, wuk_ref, kvn_ref,
                   zrw_ref, qlat_ref, ckv_ref, qidx_ref, kidx_ref, widx_ref, gate_ref):
    hn = _rms(x_ref[...], g_ref[...]).astype(BF16)
    zrw_ref[...] = jnp.dot(hn, wrw_ref[...], preferred_element_type=F32)
    zd = jnp.dot(hn, wdsa_ref[...], preferred_element_type=F32)
    o = 0
    q = zd[:, o:o + DSA_WIDTH]; o += DSA_WIDTH
    c = zd[:, o:o + DSA_KV_RANK]; o += DSA_KV_RANK
    qi = zd[:, o:o + IDX_HEADS * LANE]; o += IDX_HEADS * LANE
    ki = zd[:, o:o + LANE]; o += LANE
    wi = zd[:, o:o + LANE]
    qlat_ref[...] = (_mm(q, wuk_ref[...]) * (DSA_HEAD_DIM ** -0.5 * LOG2E)).astype(BF16)
    ckv_ref[...] = _rms(c, kvn_ref[...]).astype(BF16)
    qidx_ref[...] = qi.astype(BF16)
    kidx_ref[...] = ki.astype(BF16)
    widx_ref[...] = wi * ((IDX_HEADS * IDX_DIM) ** -0.5)
    zg = jnp.dot(hn, wgate_ref[...], preferred_element_type=F32) + bgate_ref[...]
    gate_ref[...] = _sigmoid(zg)


def _inproj(x2, norm_mix, w_rw, w_dsa, w_gate, b_gate, wuk_bd, kv_norm, tm):
    T, D = x2.shape
    grid = (T // tm,)
    row = lambda n: pl.BlockSpec((tm, n), lambda i: (i, 0))
    outs = [
        jax.ShapeDtypeStruct((T, RW_ZCOLS), F32),
        jax.ShapeDtypeStruct((T, DSA_HEADS * DSA_KV_RANK), BF16),
        jax.ShapeDtypeStruct((T, DSA_KV_RANK), BF16),
        jax.ShapeDtypeStruct((T, IDX_HEADS * LANE), BF16),
        jax.ShapeDtypeStruct((T, LANE), BF16),
        jax.ShapeDtypeStruct((T, LANE), F32),
        jax.ShapeDtypeStruct((T, 2 * D), F32),
    ]
    return pl.pallas_call(
        _inproj_kernel,
        grid=grid,
        in_specs=[row(D), _const_spec((1, D)), _const_spec(w_rw.shape), _const_spec(w_dsa.shape),
                  _const_spec(w_gate.shape), _const_spec((1, 2 * D)), _const_spec(wuk_bd.shape),
                  _const_spec((1, DSA_KV_RANK))],
        out_specs=[row(s.shape[1]) for s in outs],
        out_shape=outs,
        compiler_params=_cparams(("parallel",)),
        name="inproj",
    )(x2, norm_mix.reshape(1, D), w_rw, w_dsa, w_gate, b_gate.reshape(1, 2 * D), wuk_bd,
      kv_norm.reshape(1, DSA_KV_RANK))


def _rwkv_kernel(z_ref, mu_ref, w0_ref, w2_ref, a0_ref, a2_ref, g2_ref, kk_ref, ka_ref, rk_ref,
                 lnw_ref, lnb_ref, hs_ref, y_ref, state_ref, carry_ref, ybuf_ref):
    C, N, H = RW_CHUNK, RW_HEAD_DIM, RW_HEADS
    W = RW_WIDTH
    NB = z_ref.shape[0]

    @pl.when(pl.program_id(1) == 0)
    def _():
        state_ref[...] = jnp.zeros_like(state_ref)
        carry_ref[...] = jnp.zeros_like(carry_ref)

    hs = hs_ref[...]
    headsum = lambda t: _mm_hilo_rhs(t, hs)
    ti = lax.broadcasted_iota(I32, (C, C), 0)
    si = lax.broadcasted_iota(I32, (C, C), 1)
    ltri = (si <= ti).astype(BF16)
    t2 = lax.broadcasted_iota(I32, (C, 2 * C), 0)
    s2 = lax.broadcasted_iota(I32, (C, 2 * C), 1) % C
    m_strict = s2 < t2
    m_incl = s2 <= t2
    eye = (si == ti).astype(F32)
    zeros_cn = jnp.zeros((C, N), F32)
    row = lax.broadcasted_iota(I32, (C, RW_ZCOLS), 0)

    a_t, r_t, b_t, k_t, v, p_c, bonus, g = ([None] * NB for _ in range(8))
    for bi in range(NB):
        z = z_ref[bi]
        zprev = jnp.where(row == 0, carry_ref[bi], pltpu.roll(z, 1, 0))
        carry_ref[bi] = z[C - 1:C, :]
        zm = z + (zprev - z) * mu_ref[...]
        r = zm[:, 0:W]
        k = zm[:, W:2 * W]
        v[bi] = zm[:, 2 * W:3 * W]
        xw = zm[:, 3 * W:3 * W + LANE]
        xa = zm[:, 3 * W + LANE:3 * W + 2 * LANE]
        xg = zm[:, 3 * W + 2 * LANE:3 * W + 3 * LANE]
        wlog = -_softplus(-(w0_ref[...] + _mm(jnp.tanh(xw), w2_ref[...]))) - 0.5
        dlog = -jnp.exp(wlog)
        a = _sigmoid(a0_ref[...] + _mm(xa, a2_ref[...]))
        g[bi] = _mm(_sigmoid(xg), g2_ref[...])
        kk = k * kk_ref[...]
        kk = kk / jnp.maximum(jnp.sqrt(headsum(kk * kk)), 1e-12)
        k_eff = k * (1.0 + (a - 1.0) * ka_ref[...])
        bonus[bi] = headsum(r * k_eff * rk_ref[...]) * v[bi]
        cum = _mm_exact_lhs(ltri, dlog)
        pin = jnp.exp(cum)
        pinv = jnp.exp(-cum)
        a_t[bi] = -kk * jnp.exp(cum - dlog)
        r_t[bi] = r * pin
        b_t[bi] = kk * a * pinv
        k_t[bi] = k_eff * pinv
        p_c[bi] = pin[C - 1:C, :]

    units = [(bi, h) for bi in range(NB) for h in range(H)]
    ur = range(len(units))
    sl = lambda h: slice(h * N, (h + 1) * N)
    ar = [jnp.concatenate([a_t[bi][:, sl(h)], r_t[bi][:, sl(h)]], axis=0).astype(BF16) for bi, h in units]
    bk = [jnp.concatenate([b_t[bi][:, sl(h)], k_t[bi][:, sl(h)]], axis=0).astype(BF16) for bi, h in units]
    vh = [v[bi][:, sl(h)] for bi, h in units]
    s0 = [state_ref[bi, h] for bi, h in units]
    q = [_mm_nt(ar[u], bk[u]) for u in ur]
    w_all = [_mm_nt(ar[u], s0[u]) for u in ur]
    top = [jnp.where(m_strict, q[u][:C], 0.0).astype(BF16) for u in ur]
    bot = [jnp.where(m_incl, q[u][C:], 0.0).astype(BF16) for u in ur]
    rhs = [w_all[u][:C] + _mm(top[u], jnp.concatenate([zeros_cn, vh[u]], axis=0)) for u in ur]
    pw = [top[u][:, :C] for u in ur]
    inv = [eye + pw[u].astype(F32) for u in ur]
    for _ in range(5):
        pw = [_mm(pw[u], pw[u]).astype(BF16) for u in ur]
        inv = [inv[u] + _mm(pw[u], inv[u]) for u in ur]
    uu = [_mm(inv[u], rhs[u]) for u in ur]
    gsrc = [jnp.concatenate([uu[u], vh[u]], axis=0).astype(BF16) for u in ur]
    for u, (bi, h) in enumerate(units):
        ybuf_ref[bi, :, sl(h)] = w_all[u][C:] + _mm(bot[u], gsrc[u])
    for u, (bi, h) in enumerate(units):
        state_ref[bi, h] = (s0[u] + _mm_tn(gsrc[u], bk[u])) * p_c[bi][:, sl(h)]

    for bi in range(NB):
        y = ybuf_ref[bi]
        mean = headsum(y) * (1.0 / N)
        d = y - mean
        var = headsum(d * d) * (1.0 / N)
        yn = d * lax.rsqrt(var + RW_GN_EPS) * lnw_ref[...] + lnb_ref[...]
        y_ref[bi] = ((yn + bonus[bi]) * g[bi]).astype(y_ref.dtype)


def _rwkv(z_rw, B, S, mu, w0, w2, a0, a2, g2, k_k, k_a, r_k, ln_w, ln_b):
    C, W = RW_CHUNK, RW_WIDTH
    nb = RW_SEQS if B % RW_SEQS == 0 else 1
    nc = S // C
    hid = jnp.arange(W) // RW_HEAD_DIM
    hs = (hid[:, None] == hid[None, :]).astype(BF16)
    vec = lambda t: t.reshape(1, -1)
    consts = [vec(mu), vec(w0), w2, vec(a0), a2, g2, vec(k_k), vec(k_a), vec(r_k), vec(ln_w),
              vec(ln_b), hs]
    y = pl.pallas_call(
        _rwkv_kernel,
        grid=(B // nb, nc),
        in_specs=[pl.BlockSpec((nb, C, RW_ZCOLS), lambda b, c: (b, c, 0))]
                 + [_const_spec(t.shape) for t in consts],
        out_specs=pl.BlockSpec((nb, C, W), lambda b, c: (b, c, 0)),
        out_shape=jax.ShapeDtypeStruct((B, S, W), BF16),
        scratch_shapes=[pltpu.VMEM((nb, RW_HEADS, RW_HEAD_DIM, RW_HEAD_DIM), F32),
                        pltpu.VMEM((nb, 1, RW_ZCOLS), F32),
                        pltpu.VMEM((nb, C, W), F32)],
        compiler_params=_cparams(("parallel", "arbitrary")),
        name="rwkv",
    )(z_rw.reshape(B, S, RW_ZCOLS), *consts)
    return y.reshape(B * S, W)


def _dsa_kernel(qlat_ref, qidx_ref, widx_ref, ckv_ref, ckvt_ref, kidx_ref, wuvt_ref, y_ref,
                  keys_ref, hi_ref, lo_ref, bias_ref, m_ref, acc_ref, *, k_sel):
    TQ, TK = DSA_TQ, DSA_TK
    HQ = 256
    SUBP = 16
    I16 = jnp.int16
    HALF = 1 << 15
    R = DSA_KV_RANK
    qi = pl.program_id(1)
    nkt = (qi + 1) * (TQ // TK)
    npair = nkt // 2
    keypos = lax.broadcasted_iota(I32, (TK, TQ), 0)
    rowpos = qi * TQ + lax.broadcasted_iota(I32, (TK, TQ), 1)
    one16 = jnp.ones((), BF16)
    zero16 = jnp.zeros((), BF16)

    def score_body(kt, carry):
        kx = kidx_ref[pl.ds(pl.multiple_of(kt * TK, TK), TK), :]
        sc = jnp.zeros((TK, TQ), F32)
        for h in range(IDX_HEADS):
            rel = jnp.maximum(_mm_nt(kx, qidx_ref[:, h * LANE:(h + 1) * LANE]), 0.0)
            sc = sc + widx_ref[h:h + 1, :] * rel
        sc = jnp.where(sc == 0.0, 0.0, sc)
        bits = pltpu.bitcast(sc, I32)
        key = jnp.where(bits >= 0, bits, bits ^ jnp.int32(0x7FFFFFFF))
        key = jnp.where(kt * TK + keypos <= rowpos, key, jnp.int32(INT_MIN))
        keys_ref[kt] = key
        hi_ref[kt] = (key >> 16).astype(I16)
        return carry

    lax.fori_loop(0, nkt, score_body, 0)

    def count16(ref, level, strict):
        lv = jnp.broadcast_to(level, (SUBP, TQ)).astype(I16)

        def body(kp, acc):
            for j in range(2):
                for c in range(TK // SUBP):
                    kk = ref[2 * kp + j, c * SUBP:(c + 1) * SUBP, :]
                    acc = acc + jnp.where((kk > lv) if strict else (kk >= lv), one16, zero16)
            return acc

        acc = lax.fori_loop(0, npair, body, jnp.zeros((SUBP, TQ), BF16))
        return jnp.sum(acc.astype(F32), axis=0, keepdims=True)

    def select16(ref, k_row):
        def bit_body(i, tu):
            cand = tu | lax.shift_left(jnp.int32(1), 15 - i)
            return jnp.where(count16(ref, cand - HALF, strict=False) >= k_row, cand, tu)
        return lax.fori_loop(0, 16, bit_body, jnp.zeros((1, TQ), I32))

    th = select16(hi_ref, jnp.full((1, TQ), k_sel, F32)) - HALF
    n_gt_hi = count16(hi_ref, th, strict=True)

    def lo_body(kt, carry):
        key = keys_ref[kt]
        lo = (key & jnp.int32(0xFFFF)) - HALF
        lo_ref[kt] = jnp.where((key >> 16) == th, lo, -HALF).astype(I16)
        return carry

    lax.fori_loop(0, nkt, lo_body, 0)
    tl = select16(lo_ref, k_sel - n_gt_hi)
    n_gt = n_gt_hi + count16(lo_ref, tl - HALF, strict=True)
    thr = th * (1 << 16) + tl
    need = k_sel - n_gt

    m_ref[...] = jnp.full_like(m_ref, NEG)
    acc_ref[...] = jnp.zeros_like(acc_ref)
    lrow = lax.broadcasted_iota(I32, (TK + 8, TK), 0)
    lext = ((lax.broadcasted_iota(I32, (TK + 8, TK), 1) < lrow) | (lrow >= TK)).astype(BF16)

    def att_body(kp, eqc):
        for j in range(2):
            kt = 2 * kp + j
            key = keys_ref[kt]
            eq = key == thr
            pre = jnp.dot(lext, eq.astype(F32).astype(BF16), preferred_element_type=F32)
            sel = ((key > thr) | (eq & (pre[:TK] + eqc < need))) & (kt * TK + keypos <= rowpos)
            bias_ref[j * TK:(j + 1) * TK, :] = jnp.where(sel, 0.0, NEG)
            eqc = eqc + pre[TK:TK + 1]
        ck = ckv_ref[pl.ds(pl.multiple_of(kp * (2 * TK), 2 * TK), 2 * TK), :]
        ctt = ckvt_ref[kp]
        for h in range(DSA_HEADS):
            for r0 in range(0, TQ, HQ):
                s = _mm_nt(ck, qlat_ref[r0:r0 + HQ, h * R:(h + 1) * R]) + bias_ref[:, r0:r0 + HQ]
                m_old = m_ref[h, :, r0:r0 + HQ]
                m_new = jnp.maximum(m_old, jnp.max(s, axis=0, keepdims=True))
                p = jnp.exp2(s - m_new[0:1])
                acc_ref[h, :, r0:r0 + HQ] = (jnp.exp2(m_old - m_new)[0:1] * acc_ref[h, :, r0:r0 + HQ]
                                             + jnp.dot(ctt, p.astype(BF16), preferred_element_type=F32))
                m_ref[h, :, r0:r0 + HQ] = m_new
        return eqc

    lax.fori_loop(0, npair, att_body, jnp.zeros((1, TQ), F32))
    o = jnp.concatenate([(acc_ref[h, :R, :] / acc_ref[h, R:R + 1, :]).astype(BF16)
                         for h in range(DSA_HEADS)], axis=0)
    y_ref[...] = jnp.dot(wuvt_ref[...], o, preferred_element_type=F32).astype(y_ref.dtype)


def _dsa(qlat, qidx, widx, ckv, kidx, wuv_bd, B, S):
    TQ, TK = DSA_TQ, DSA_TK
    R = DSA_KV_RANK
    nq = S // TQ
    nk = S // TK
    assert TQ == 2 * TK and S % TQ == 0 and nk * TK // 16 <= 256
    k_sel = min(IDX_TOPK_MAX, S // 4)
    widx_t = widx[:, :8].T
    ckv_t = jnp.concatenate([ckv.reshape(B * nq, 2 * TK, R).transpose(0, 2, 1),
                             jnp.ones((B * nq, DSA_ONES, 2 * TK), ckv.dtype)], axis=1)
    qrow = lambda n: pl.BlockSpec((TQ, n), lambda b, q: (b * nq + q, 0))
    y_t = pl.pallas_call(
        functools.partial(_dsa_kernel, k_sel=k_sel),
        grid=(B, nq),
        in_specs=[qrow(DSA_HEADS * R), qrow(IDX_HEADS * LANE),
                  pl.BlockSpec((8, TQ), lambda b, q: (0, b * nq + q)),
                  pl.BlockSpec((S, R), lambda b, q: (b, 0)),
                  pl.BlockSpec((nq, R + DSA_ONES, 2 * TK), lambda b, q: (b, 0, 0)),
                  pl.BlockSpec((S, LANE), lambda b, q: (b, 0)),
                  _const_spec((DSA_WIDTH, DSA_HEADS * R))],
        out_specs=pl.BlockSpec((None, DSA_WIDTH, TQ), lambda b, q: (b, 0, q)),
        out_shape=jax.ShapeDtypeStruct((B, DSA_WIDTH, S), BF16),
        scratch_shapes=[pltpu.VMEM((nk, TK, TQ), I32), pltpu.VMEM((nk, TK, TQ), jnp.int16),
                        pltpu.VMEM((nk, TK, TQ), jnp.int16),
                        pltpu.VMEM((2 * TK, TQ), F32),
                        pltpu.VMEM((DSA_HEADS, 8, TQ), F32),
                        pltpu.VMEM((DSA_HEADS, R + DSA_ONES, TQ), F32)],
        compiler_params=_cparams(("parallel", "arbitrary")),
        name="dsa",
    )(qlat, qidx, widx_t, ckv, ckv_t, kidx, wuv_bd.T)
    return y_t.transpose(0, 2, 1).reshape(B * S, DSA_WIDTH)


def _memkv_kernel(m_ref, g_ref, w_ref, o_ref):
    o_ref[...] = _mm(_rms(m_ref[...], g_ref[...]), w_ref[...]).astype(o_ref.dtype)


def _memkv(mem2, norm_mem, w_ckv, M):
    R, D = mem2.shape
    return pl.pallas_call(
        _memkv_kernel,
        grid=(R // M,),
        in_specs=[pl.BlockSpec((M, D), lambda i: (i, 0)), _const_spec((1, D)), _const_spec(w_ckv.shape)],
        out_specs=pl.BlockSpec((M, w_ckv.shape[1]), lambda i: (i, 0)),
        out_shape=jax.ShapeDtypeStruct((R, w_ckv.shape[1]), BF16),
        compiler_params=_cparams(("parallel",)),
        name="memkv",
    )(mem2, norm_mem.reshape(1, D), w_ckv)


def _mix_kernel(x_ref, ya_ref, yb_ref, gate_ref, kv_ref, wpa_ref, wpb_ref, wout_ref, ncross_ref,
                wcq_ref, wco_ref, nffn_ref, wrh_ref, wrl_ref, br_ref,
                h2_ref, hn3_ref, ids_ref, gts_ref):
    D = x_ref.shape[1]
    xw = D // X_HEADS
    a = jnp.dot(ya_ref[...], wpa_ref[...], preferred_element_type=F32)
    b = jnp.dot(yb_ref[...], wpb_ref[...], preferred_element_type=F32)
    mix = gate_ref[:, :D] * a + gate_ref[:, D:] * b
    h1 = x_ref[...] + _mm(mix, wout_ref[...])
    q = _mm(_rms(h1, ncross_ref[...]), wcq_ref[...]).astype(BF16)
    outs = []
    for h in range(X_HEADS):
        s = _mm_nt(q[:, h * xw:(h + 1) * xw], kv_ref[:, h * xw:(h + 1) * xw]) * (xw ** -0.5)
        e = jnp.exp(s - jnp.max(s, axis=-1, keepdims=True))
        p = e / jnp.sum(e, axis=-1, keepdims=True)
        outs.append(_mm(p, kv_ref[:, D + h * xw:D + (h + 1) * xw]).astype(BF16))
    h2 = h1 + jnp.dot(jnp.concatenate(outs, axis=-1), wco_ref[...], preferred_element_type=F32)
    h2_ref[...] = h2
    hn3 = _rms(h2, nffn_ref[...])
    _store_row_tiles(hn3_ref, hn3)

    hi = hn3.astype(BF16)
    lo = (hn3 - hi.astype(F32)).astype(BF16)
    lg = (jnp.dot(hi, wrh_ref[...], preferred_element_type=F32)
          + jnp.dot(lo, wrh_ref[...], preferred_element_type=F32)
          + jnp.dot(hi, wrl_ref[...], preferred_element_type=F32)) + br_ref[...]
    lane = lax.broadcasted_iota(I32, lg.shape, 1)
    big = jnp.int32(1 << 20)
    rmax = lambda t: jnp.max(t, axis=-1, keepdims=True)
    rsum = lambda t: jnp.sum(t, axis=-1, keepdims=True)
    first = lambda hit: jnp.min(jnp.where(hit, lane, big), axis=-1, keepdims=True)
    gmask = lane < N_GROUPS
    gl = jnp.where(gmask, lg, NEG)
    ge = jnp.where(gmask, jnp.exp(gl - rmax(gl)), 0.0)
    gp = jnp.where(gmask, ge / rsum(ge), -1.0)
    p_grp = rmax(gp)
    grp = first(gp == p_grp)
    eidx = lane - N_GROUPS
    emask = (eidx >= 0) & (eidx < N_EXPERTS) & ((eidx // EXPERTS_PER_GROUP) == grp)
    el = jnp.where(emask, lg, NEG)
    ee = jnp.where(emask, jnp.exp(el - rmax(el)), 0.0)
    ep = jnp.where(emask, ee / rsum(ee), -1.0)
    p1 = rmax(ep)
    i1 = first(ep == p1)
    ep2 = jnp.where(lane == i1, -1.0, ep)
    p2 = rmax(ep2)
    i2 = first(ep2 == p2)
    den = p1 + p2
    ids_ref[...] = jnp.where(lane == 0, i1 - N_GROUPS, jnp.where(lane == 1, i2 - N_GROUPS, 0))
    gts_ref[...] = jnp.where(lane == 0, p_grp * p1 / den, jnp.where(lane == 1, p_grp * p2 / den, 0.0))


def _mix(x2, ya, yb, gates, kv, wpa, wpb, wout, ncross, wcq, wco, nffn, wr_hi, wr_lo, br, B, S, M, tm):
    T, D = x2.shape
    nt = S // tm
    row = lambda n: pl.BlockSpec((tm, n), lambda i: (i, 0))
    consts = [wpa, wpb, wout, ncross.reshape(1, D), wcq, wco, nffn.reshape(1, D), wr_hi, wr_lo, br]
    outs = [jax.ShapeDtypeStruct((T, D), F32), jax.ShapeDtypeStruct((T * (D // LANE), LANE), F32),
            jax.ShapeDtypeStruct((T, LANE), I32), jax.ShapeDtypeStruct((T, LANE), F32)]
    return pl.pallas_call(
        _mix_kernel,
        grid=(T // tm,),
        in_specs=[row(D), row(RW_WIDTH), row(DSA_WIDTH), row(2 * D),
                  pl.BlockSpec((M, 2 * D), lambda i: (i // nt, 0))]
                 + [_const_spec(t.shape) for t in consts],
        out_specs=[row(D), pl.BlockSpec((tm * (D // LANE), LANE), lambda i: (i, 0)), row(LANE), row(LANE)],
        out_shape=outs,
        compiler_params=_cparams(("parallel",)),
        name="mix_cross_router",
    )(x2, ya, yb, gates, kv, *consts)


def _expert_kernel(be_ref, nu_ref, tok_ref, ntok_ref, x_hbm, wg_ref, wu_ref, wd_ref, o_ref, xbuf_ref, sem):
    i = pl.program_id(0)
    n_used = nu_ref[0]
    cur = i % 2
    sub = wg_ref.shape[0] // LANE
    tile = lambda ref, r: ref.at[pl.ds(pl.multiple_of(r * sub, sub), sub), :]

    def gather(t_ref, b):
        def start(r, c):
            pltpu.make_async_copy(tile(x_hbm, t_ref[0, 0, r]), tile(xbuf_ref.at[b], r), sem.at[b]).start()
            return c
        lax.fori_loop(0, MOE_BLOCK, start, 0, unroll=4)

    @pl.when(i == 0)
    def _():
        gather(tok_ref, 0)

    @pl.when(i + 1 < n_used)
    def _():
        gather(ntok_ref, 1 - cur)

    @pl.when(i < n_used)
    def _():
        pltpu.make_async_copy(x_hbm.at[pl.ds(0, MOE_BLOCK * sub), :], xbuf_ref.at[cur], sem.at[cur]).wait()
        x = _load_row_tiles(xbuf_ref.at[cur], MOE_BLOCK, sub).astype(BF16)
        gt = jnp.dot(x, wg_ref[...], preferred_element_type=F32)
        up = jnp.dot(x, wu_ref[...], preferred_element_type=F32)
        hid = gt * _sigmoid(gt) * up
        _store_row_tiles(o_ref, _mm(hid, wd_ref[...]))

    @pl.when(i >= n_used)
    def _():
        o_ref[...] = jnp.zeros_like(o_ref)


def _experts(hn, tok_of_slot, blk_expert, n_used, w_gate, w_up, w_down):
    E, D, F = w_gate.shape
    sub = D // LANE
    P = tok_of_slot.shape[0]
    nb = P // MOE_BLOCK
    tok3 = tok_of_slot.reshape(nb, 1, MOE_BLOCK)
    grid_spec = pltpu.PrefetchScalarGridSpec(
        num_scalar_prefetch=2,
        grid=(nb,),
        in_specs=[pl.BlockSpec((1, 1, MOE_BLOCK), lambda i, be, nu: (i, 0, 0), memory_space=pltpu.SMEM),
                  pl.BlockSpec((1, 1, MOE_BLOCK), lambda i, be, nu: (jnp.minimum(i + 1, nb - 1), 0, 0),
                               memory_space=pltpu.SMEM),
                  pl.BlockSpec(memory_space=pl.ANY),
                  pl.BlockSpec((None, D, F), lambda i, be, nu: (be[i], 0, 0)),
                  pl.BlockSpec((None, D, F), lambda i, be, nu: (be[i], 0, 0)),
                  pl.BlockSpec((None, F, D), lambda i, be, nu: (be[i], 0, 0))],
        out_specs=pl.BlockSpec((MOE_BLOCK * sub, LANE), lambda i, be, nu: (i, 0)),
        scratch_shapes=[pltpu.VMEM((2, MOE_BLOCK * sub, LANE), F32), pltpu.SemaphoreType.DMA((2,))],
    )
    return pl.pallas_call(
        _expert_kernel,
        grid_spec=grid_spec,
        out_shape=jax.ShapeDtypeStruct((P * sub, LANE), F32),
        compiler_params=_cparams(("arbitrary",)),
        name="experts",
    )(blk_expert, n_used, tok3, tok3, hn, w_gate, w_up, w_down)


def _final_kernel(slot_ref, nslot_ref, h_ref, gts_ref, g_ref, ys_ref, o_ref, buf_ref, sem, *, apply_norm):
    n_tok, d = h_ref.shape
    sub = d // LANE
    tile = lambda ref, r: ref.at[pl.ds(pl.multiple_of(r * sub, sub), sub), :]
    i = pl.program_id(0)
    cur = i % 2

    def gather(s_ref, b):
        def start(t, c):
            for j in range(2):
                pltpu.make_async_copy(tile(ys_ref, s_ref[0, 0, 2 * t + j]), tile(buf_ref.at[b, j], t),
                                      sem.at[b]).start()
            return c
        lax.fori_loop(0, n_tok, start, 0, unroll=4)

    @pl.when(i == 0)
    def _():
        gather(slot_ref, 0)

    @pl.when(i + 1 < pl.num_programs(0))
    def _():
        gather(nslot_ref, 1 - cur)

    for j in range(2):
        pltpu.make_async_copy(ys_ref.at[pl.ds(0, n_tok * sub), :], buf_ref.at[cur, j], sem.at[cur]).wait()
    y = (gts_ref[:, 0:1] * _load_row_tiles(buf_ref.at[cur, 0], n_tok, sub)
         + gts_ref[:, 1:2] * _load_row_tiles(buf_ref.at[cur, 1], n_tok, sub))
    h = h_ref[...] + y
    o_ref[...] = _rms(h, g_ref[...]) if apply_norm else h


def _final(h2, ys, slots, gts, norm_final, apply_norm):
    T, D = h2.shape
    tm = ROW_TOK
    nt = T // tm
    row = lambda n: pl.BlockSpec((tm, n), lambda i: (i, 0))
    slots3 = slots.reshape(nt, 1, 2 * tm)
    return pl.pallas_call(
        functools.partial(_final_kernel, apply_norm=apply_norm),
        grid=(nt,),
        in_specs=[pl.BlockSpec((1, 1, 2 * tm), lambda i: (i, 0, 0), memory_space=pltpu.SMEM),
                  pl.BlockSpec((1, 1, 2 * tm), lambda i: (jnp.minimum(i + 1, nt - 1), 0, 0),
                               memory_space=pltpu.SMEM),
                  row(D), row(LANE), _const_spec((1, D)), pl.BlockSpec(memory_space=pl.ANY)],
        out_specs=row(D),
        out_shape=jax.ShapeDtypeStruct((T, D), F32),
        scratch_shapes=[pltpu.VMEM((2, 2, tm * (D // LANE), LANE), F32), pltpu.SemaphoreType.DMA((2,))],
        compiler_params=_cparams(("arbitrary",)),
        name="final",
    )(slots3, slots3, h2, gts, norm_final.reshape(1, D), ys)


def _pad_cols(w, n):
    return jnp.pad(w, ((0, 0), (0, n - w.shape[1])))


def _pad_rows(w, n):
    return jnp.pad(w, ((0, n - w.shape[0]), (0, 0)))


def _layer(h, mem, norm_mix, w_in, shift_mu, rw_w0, rw_w2, rw_a0, rw_a2, rw_g2, rw_k_k, rw_k_a,
           rw_r_k, rw_ln_w, rw_ln_b, kv_norm, w_uk, w_uv, w_proj_a, w_proj_b, b_gate, w_out,
           norm_cross, norm_mem, w_cq, w_ckv, w_co, norm_ffn, w_router_g, b_router_g, w_router_e,
           b_router_e, w_e_gate, w_e_up, w_e_down):
    B, S, D = h.shape
    M = mem.shape[1]
    T = B * S
    W = RW_WIDTH
    x2 = h.reshape(T, D)

    c = 0
    pieces = {}
    for name, n in (("r", W), ("k", W), ("v", W), ("xw", RW_DECAY_LORA), ("xa", RW_AAA_LORA),
                    ("xg", RW_GATE_LORA), ("q", DSA_WIDTH), ("ckv", DSA_KV_RANK),
                    ("qidx", IDX_HEADS * IDX_DIM), ("kidx", IDX_DIM), ("widx", IDX_HEADS),
                    ("gate", 2 * D)):
        pieces[name] = (c, n)
        c += n
    col = lambda name: w_in[:, pieces[name][0]:pieces[name][0] + pieces[name][1]]
    mu_of = lambda name: shift_mu[pieces[name][0]:pieces[name][0] + pieces[name][1]]
    w_rw = jnp.concatenate([col("r"), col("k"), col("v"), _pad_cols(col("xw"), LANE),
                            _pad_cols(col("xa"), LANE), _pad_cols(col("xg"), LANE)], axis=1).astype(BF16)
    padv = lambda t: jnp.pad(t, (0, LANE - t.shape[0]))
    mu = jnp.concatenate([mu_of("r"), mu_of("k"), mu_of("v"), padv(mu_of("xw")), padv(mu_of("xa")),
                          padv(mu_of("xg"))])
    qidx_w = col("qidx").reshape(D, IDX_HEADS, IDX_DIM)
    qidx_w = jnp.pad(qidx_w, ((0, 0), (0, 0), (0, LANE - IDX_DIM))).reshape(D, IDX_HEADS * LANE)
    w_dsa = jnp.concatenate([col("q"), col("ckv"), qidx_w, _pad_cols(col("kidx"), LANE),
                             _pad_cols(col("widx"), LANE)], axis=1).astype(BF16)
    w_gate_in = col("gate").astype(BF16)
    eye_h = jnp.eye(DSA_HEADS, dtype=F32)
    wuk_bd = jnp.einsum("rhd,hg->hdgr", w_uk, eye_h).reshape(DSA_WIDTH, DSA_HEADS * DSA_KV_RANK).astype(BF16)
    wuv_bd = jnp.einsum("rhd,hg->hrgd", w_uv, eye_h).reshape(DSA_HEADS * DSA_KV_RANK, DSA_WIDTH).astype(BF16)

    tm1 = min(ROW_BLOCK, T)
    z_rw, qlat, ckv, qidx, kidx, widx, gates = _inproj(
        x2, norm_mix, w_rw, w_dsa, w_gate_in, b_gate, wuk_bd, kv_norm, tm1)

    y_a = _rwkv(z_rw, B, S, mu, rw_w0, _pad_rows(rw_w2, LANE).astype(BF16), rw_a0,
                _pad_rows(rw_a2, LANE).astype(BF16), rw_g2.astype(BF16), rw_k_k, rw_k_a, rw_r_k,
                rw_ln_w, rw_ln_b)
    y_b = _dsa(qlat, qidx, widx, ckv, kidx, wuv_bd, B, S)
    kv = _memkv(mem.reshape(B * M, D), norm_mem, w_ckv.astype(BF16), M)

    w_r = jnp.concatenate([w_router_g, w_router_e], axis=1)
    w_r = _pad_cols(w_r, LANE)
    wr_hi = w_r.astype(BF16)
    wr_lo = (w_r - wr_hi.astype(F32)).astype(BF16)
    b_r = jnp.pad(jnp.concatenate([b_router_g, b_router_e]), (0, LANE - N_GROUPS - N_EXPERTS)).reshape(1, LANE)
    tm5 = min(ROW_BLOCK, S)
    h2, hn3, ids, gts = _mix(x2, y_a, y_b, gates, kv, w_proj_a.astype(BF16), w_proj_b.astype(BF16),
                             w_out.astype(BF16), norm_cross, w_cq.astype(BF16), w_co.astype(BF16),
                             norm_ffn, wr_hi, wr_lo, b_r, B, S, M, tm5)

    A = 2 * T
    e_flat = ids[:, :2].reshape(A)
    onehot = (e_flat[:, None] == jnp.arange(N_EXPERTS, dtype=I32)[None, :]).astype(I32)
    csum = jnp.cumsum(onehot, axis=0)
    rank = jnp.sum(csum * onehot, axis=1) - 1
    counts = csum[-1]
    padded = (counts + MOE_BLOCK - 1) // MOE_BLOCK * MOE_BLOCK
    pend = jnp.cumsum(padded)
    pstart = pend - padded
    slots = (jnp.sum(pstart[None, :] * onehot, axis=1) + rank).astype(I32)
    P = A + N_EXPERTS * MOE_BLOCK
    nb = P // MOE_BLOCK
    blk_pos = jnp.arange(nb, dtype=I32) * MOE_BLOCK
    blk_expert = jnp.minimum(jnp.sum((pend[None, :] <= blk_pos[:, None]).astype(I32), axis=1),
                             N_EXPERTS - 1).astype(I32)
    n_used = (pend[-1:] // MOE_BLOCK).astype(I32)
    order = jnp.argsort(e_flat, stable=True).astype(I32)
    start = jnp.cumsum(counts) - counts
    slot_e = jnp.repeat(blk_expert, MOE_BLOCK)
    slot_r = jnp.arange(P, dtype=I32) - pstart[slot_e]
    slot_ok = slot_r < counts[slot_e]
    tok_of_slot = jnp.where(slot_ok, order[jnp.clip(start[slot_e] + slot_r, 0, A - 1)] // 2, 0).astype(I32)

    ys = _experts(hn3, tok_of_slot, blk_expert, n_used, w_e_gate.astype(BF16), w_e_up.astype(BF16),
                  w_e_down.astype(BF16))
    return h2, ys, slots, gts


def kernel(x, mem, norm_mix, w_in, shift_mu, rw_w0, rw_w2, rw_a0, rw_a2, rw_g2, rw_k_k, rw_k_a, rw_r_k, rw_ln_w, rw_ln_b, kv_norm, w_uk, w_uv, w_proj_a, w_proj_b, b_gate, w_out, norm_cross, norm_mem, w_cq, w_ckv, w_co, norm_ffn, w_router_g, b_router_g, w_router_e, b_router_e, w_e_gate, w_e_up, w_e_down, norm_final):
    B, S, D = x.shape
    depth = norm_mix.shape[0]
    h = x
    for l in range(depth):
        last = l == depth - 1
        h2, ys, slots, gts = _layer(
            h, mem, norm_mix[l], w_in[l], shift_mu[l], rw_w0[l], rw_w2[l], rw_a0[l], rw_a2[l],
            rw_g2[l], rw_k_k[l], rw_k_a[l], rw_r_k[l], rw_ln_w[l], rw_ln_b[l], kv_norm[l], w_uk[l],
            w_uv[l], w_proj_a[l], w_proj_b[l], b_gate[l], w_out[l], norm_cross[l], norm_mem[l],
            w_cq[l], w_ckv[l], w_co[l], norm_ffn[l], w_router_g[l], b_router_g[l], w_router_e[l],
            b_router_e[l], w_e_gate[l], w_e_up[l], w_e_down[l])
        h = _final(h2, ys, slots, gts, norm_final, last).reshape(B, S, D)
    return h
```

```python
import functools

import jax
import jax.numpy as jnp
from jax import lax
from jax.experimental import pallas as pl
from jax.experimental.pallas import tpu as pltpu

F32 = jnp.float32
BF16 = jnp.bfloat16
I32 = jnp.int32

RMS_EPS = 1e-6
RW_HEADS = 8
RW_HEAD_DIM = 64
RW_WIDTH = RW_HEADS * RW_HEAD_DIM
RW_DECAY_LORA = 64
RW_AAA_LORA = 64
RW_GATE_LORA = 128
RW_GN_EPS = 64e-5
DSA_HEADS = 8
DSA_HEAD_DIM = 64
DSA_WIDTH = DSA_HEADS * DSA_HEAD_DIM
DSA_KV_RANK = 128
IDX_HEADS = 4
IDX_DIM = 64
IDX_TOPK_MAX = 256
X_HEADS = 4
N_GROUPS = 4
EXPERTS_PER_GROUP = 8
N_EXPERTS = N_GROUPS * EXPERTS_PER_GROUP
MOE_BLOCK = 256

LANE = 128
RW_CHUNK = 64
RW_SEQS = 4
RW_ZCOLS = 3 * RW_WIDTH + 3 * LANE
DSA_ZCOLS = DSA_WIDTH + DSA_KV_RANK + IDX_HEADS * LANE + LANE + LANE
DSA_TQ = 512
DSA_TK = 256
DSA_ONES = 128
ROW_BLOCK = 512
ROW_TOK = 256
NEG = -1e30
LOG2E = 1.4426950408889634
INT_MIN = -(2 ** 31)
VMEM_LIMIT = 56 * 1024 * 1024


def _mm(a, b):
    return jnp.dot(a.astype(BF16), b.astype(BF16), preferred_element_type=F32)


def _mm_nt(a, b):
    return lax.dot_general(a.astype(BF16), b.astype(BF16), (((1,), (1,)), ((), ())),
                           preferred_element_type=F32)


def _mm_tn(a, b):
    return lax.dot_general(a.astype(BF16), b.astype(BF16), (((0,), (0,)), ((), ())),
                           preferred_element_type=F32)


def _split3(x):
    hi = x.astype(BF16)
    r1 = x - hi.astype(F32)
    mid = r1.astype(BF16)
    lo = (r1 - mid.astype(F32)).astype(BF16)
    return hi, mid, lo


def _mm_hilo_rhs(x, b):
    hi = x.astype(BF16)
    lo = (x - hi.astype(F32)).astype(BF16)
    return jnp.dot(hi, b, preferred_element_type=F32) + jnp.dot(lo, b, preferred_element_type=F32)


def _mm_exact_lhs(a, x):
    hi, mid, lo = _split3(x)
    return (jnp.dot(a, hi, preferred_element_type=F32) + jnp.dot(a, mid, preferred_element_type=F32)
            + jnp.dot(a, lo, preferred_element_type=F32))


def _rms(x, g):
    return x * lax.rsqrt(jnp.mean(x * x, axis=-1, keepdims=True) + RMS_EPS) * g


def _sigmoid(x):
    return 1.0 / (1.0 + jnp.exp(-x))


def _softplus(x):
    return jnp.maximum(x, 0.0) + jnp.log1p(jnp.exp(-jnp.abs(x)))


def _load_row_tiles(ref, rows, sub):
    return jnp.concatenate([ref[pl.ds(c, rows, stride=sub), :] for c in range(sub)], axis=-1)


def _store_row_tiles(ref, val):
    rows, d = val.shape
    sub = d // LANE
    for c in range(sub):
        ref[pl.ds(c, rows, stride=sub), :] = val[:, c * LANE:(c + 1) * LANE]


def _cparams(sem):
    return pltpu.CompilerParams(dimension_semantics=sem, vmem_limit_bytes=VMEM_LIMIT)


def _const_spec(shape):
    nd = len(shape)
    return pl.BlockSpec(shape, lambda *_: (0,) * nd, pipeline_mode=pl.Buffered(1))


def _inproj_kernel(x_ref, g_ref, wrw_ref, wdsa_ref, wgate_ref, bgate_ref, wuk_ref, kvn_ref,
                   zrw_ref, qlat_ref, ckv_ref, qidx_ref, kidx_ref, widx_ref, gate_ref):
    hn = _rms(x_ref[...], g_ref[...]).astype(BF16)
    zrw_ref[...] = jnp.dot(hn, wrw_ref[...], preferred_element_type=F32)
    zd = jnp.dot(hn, wdsa_ref[...], preferred_element_type=F32)
    o = 0
    q = zd[:, o:o + DSA_WIDTH]; o += DSA_WIDTH
    c = zd[:, o:o + DSA_KV_RANK]; o += DSA_KV_RANK
    qi = zd[:, o:o + IDX_HEADS * LANE]; o += IDX_HEADS * LANE
    ki = zd[:, o:o + LANE]; o += LANE
    wi = zd[:, o:o + LANE]
    qlat_ref[...] = (_mm(q, wuk_ref[...]) * (DSA_HEAD_DIM ** -0.5 * LOG2E)).astype(BF16)
    ckv_ref[...] = _rms(c, kvn_ref[...]).astype(BF16)
    qidx_ref[...] = qi.astype(BF16)
    kidx_ref[...] = ki.astype(BF16)
    widx_ref[...] = wi * ((IDX_HEADS * IDX_DIM) ** -0.5)
    zg = jnp.dot(hn, wgate_ref[...], preferred_element_type=F32) + bgate_ref[...]
    gate_ref[...] = _sigmoid(zg)


def _inproj(x2, norm_mix, w_rw, w_dsa, w_gate, b_gate, wuk_bd, kv_norm, tm):
    T, D = x2.shape
    grid = (T // tm,)
    row = lambda n: pl.BlockSpec((tm, n), lambda i: (i, 0))
    outs = [
        jax.ShapeDtypeStruct((T, RW_ZCOLS), F32),
        jax.ShapeDtypeStruct((T, DSA_HEADS * DSA_KV_RANK), BF16),
        jax.ShapeDtypeStruct((T, DSA_KV_RANK), BF16),
        jax.ShapeDtypeStruct((T, IDX_HEADS * LANE), BF16),
        jax.ShapeDtypeStruct((T, LANE), BF16),
        jax.ShapeDtypeStruct((T, LANE), F32),
        jax.ShapeDtypeStruct((T, 2 * D), F32),
    ]
    return pl.pallas_call(
        _inproj_kernel,
        grid=grid,
        in_specs=[row(D), _const_spec((1, D)), _const_spec(w_rw.shape), _const_spec(w_dsa.shape),
                  _const_spec(w_gate.shape), _const_spec((1, 2 * D)), _const_spec(wuk_bd.shape),
                  _const_spec((1, DSA_KV_RANK))],
        out_specs=[row(s.shape[1]) for s in outs],
        out_shape=outs,
        compiler_params=_cparams(("parallel",)),
        name="inproj",
    )(x2, norm_mix.reshape(1, D), w_rw, w_dsa, w_gate, b_gate.reshape(1, 2 * D), wuk_bd,
      kv_norm.reshape(1, DSA_KV_RANK))


def _rwkv_kernel(z_ref, mu_ref, w0_ref, w2_ref, a0_ref, a2_ref, g2_ref, kk_ref, ka_ref, rk_ref,
                 lnw_ref, lnb_ref, hs_ref, y_ref, state_ref, carry_ref, ybuf_ref):
    C, N, H = RW_CHUNK, RW_HEAD_DIM, RW_HEADS
    W = RW_WIDTH
    NB = z_ref.shape[0]

    @pl.when(pl.program_id(1) == 0)
    def _():
        state_ref[...] = jnp.zeros_like(state_ref)
        carry_ref[...] = jnp.zeros_like(carry_ref)

    hs = hs_ref[...]
    headsum = lambda t: _mm_hilo_rhs(t, hs)
    ti = lax.broadcasted_iota(I32, (C, C), 0)
    si = lax.broadcasted_iota(I32, (C, C), 1)
    ltri = (si <= ti).astype(BF16)
    t2 = lax.broadcasted_iota(I32, (C, 2 * C), 0)
    s2 = lax.broadcasted_iota(I32, (C, 2 * C), 1) % C
    m_strict = s2 < t2
    m_incl = s2 <= t2
    eye = (si == ti).astype(F32)
    zeros_cn = jnp.zeros((C, N), F32)
    row = lax.broadcasted_iota(I32, (C, RW_ZCOLS), 0)

    a_t, r_t, b_t, k_t, v, p_c, bonus, g = ([None] * NB for _ in range(8))
    for bi in range(NB):
        z = z_ref[bi]
        zprev = jnp.where(row == 0, carry_ref[bi], pltpu.roll(z, 1, 0))
        carry_ref[bi] = z[C - 1:C, :]
        zm = z + (zprev - z) * mu_ref[...]
        r = zm[:, 0:W]
        k = zm[:, W:2 * W]
        v[bi] = zm[:, 2 * W:3 * W]
        xw = zm[:, 3 * W:3 * W + LANE]
        xa = zm[:, 3 * W + LANE:3 * W + 2 * LANE]
        xg = zm[:, 3 * W + 2 * LANE:3 * W + 3 * LANE]
        wlog = -_softplus(-(w0_ref[...] + _mm(jnp.tanh(xw), w2_ref[...]))) - 0.5
        dlog = -jnp.exp(wlog)
        a = _sigmoid(a0_ref[...] + _mm(xa, a2_ref[...]))
        g[bi] = _mm(_sigmoid(xg), g2_ref[...])
        kk = k * kk_ref[...]
        kk = kk / jnp.maximum(jnp.sqrt(headsum(kk * kk)), 1e-12)
        k_eff = k * (1.0 + (a - 1.0) * ka_ref[...])
        bonus[bi] = headsum(r * k_eff * rk_ref[...]) * v[bi]
        cum = _mm_exact_lhs(ltri, dlog)
        pin = jnp.exp(cum)
        pinv = jnp.exp(-cum)
        a_t[bi] = -kk * jnp.exp(cum - dlog)
        r_t[bi] = r * pin
        b_t[bi] = kk * a * pinv
        k_t[bi] = k_eff * pinv
        p_c[bi] = pin[C - 1:C, :]

    units = [(bi, h) for bi in range(NB) for h in range(H)]
    ur = range(len(units))
    sl = lambda h: slice(h * N, (h + 1) * N)
    ar = [jnp.concatenate([a_t[bi][:, sl(h)], r_t[bi][:, sl(h)]], axis=0).astype(BF16) for bi, h in units]
    bk = [jnp.concatenate([b_t[bi][:, sl(h)], k_t[bi][:, sl(h)]], axis=0).astype(BF16) for bi, h in units]
    vh = [v[bi][:, sl(h)] for bi, h in units]
    s0 = [state_ref[bi, h] for bi, h in units]
    q = [_mm_nt(ar[u], bk[u]) for u in ur]
    w_all = [_mm_nt(ar[u], s0[u]) for u in ur]
    top = [jnp.where(m_strict, q[u][:C], 0.0).astype(BF16) for u in ur]
    bot = [jnp.where(m_incl, q[u][C:], 0.0).astype(BF16) for u in ur]
    rhs = [w_all[u][:C] + _mm(top[u], jnp.concatenate([zeros_cn, vh[u]], axis=0)) for u in ur]
    pw = [top[u][:, :C] for u in ur]
    inv = [eye + pw[u].astype(F32) for u in ur]
    for _ in range(5):
        pw = [_mm(pw[u], pw[u]).astype(BF16) for u in ur]
        inv = [inv[u] + _mm(pw[u], inv[u]) for u in ur]
    uu = [_mm(inv[u], rhs[u]) for u in ur]
    gsrc = [jnp.concatenate([uu[u], vh[u]], axis=0).astype(BF16) for u in ur]
    for u, (bi, h) in enumerate(units):
        ybuf_ref[bi, :, sl(h)] = w_all[u][C:] + _mm(bot[u], gsrc[u])
    for u, (bi, h) in enumerate(units):
        state_ref[bi, h] = (s0[u] + _mm_tn(gsrc[u], bk[u])) * p_c[bi][:, sl(h)]

    for bi in range(NB):
        y = ybuf_ref[bi]
        mean = headsum(y) * (1.0 / N)
        d = y - mean
        var = headsum(d * d) * (1.0 / N)
        yn = d * lax.rsqrt(var + RW_GN_EPS) * lnw_ref[...] + lnb_ref[...]
        y_ref[bi] = ((yn + bonus[bi]) * g[bi]).astype(y_ref.dtype)


def _rwkv(z_rw, B, S, mu, w0, w2, a0, a2, g2, k_k, k_a, r_k, ln_w, ln_b):
    C, W = RW_CHUNK, RW_WIDTH
    nb = RW_SEQS if B % RW_SEQS == 0 else 1
    nc = S // C
    hid = jnp.arange(W) // RW_HEAD_DIM
    hs = (hid[:, None] == hid[None, :]).astype(BF16)
    vec = lambda t: t.reshape(1, -1)
    consts = [vec(mu), vec(w0), w2, vec(a0), a2, g2, vec(k_k), vec(k_a), vec(r_k), vec(ln_w),
              vec(ln_b), hs]
    y = pl.pallas_call(
        _rwkv_kernel,
        grid=(B // nb, nc),
        in_specs=[pl.BlockSpec((nb, C, RW_ZCOLS), lambda b, c: (b, c, 0))]
                 + [_const_spec(t.shape) for t in consts],
        out_specs=pl.BlockSpec((nb, C, W), lambda b, c: (b, c, 0)),
        out_shape=jax.ShapeDtypeStruct((B, S, W), BF16),
        scratch_shapes=[pltpu.VMEM((nb, RW_HEADS, RW_HEAD_DIM, RW_HEAD_DIM), F32),
                        pltpu.VMEM((nb, 1, RW_ZCOLS), F32),
                        pltpu.VMEM((nb, C, W), F32)],
        compiler_params=_cparams(("parallel", "arbitrary")),
        name="rwkv",
    )(z_rw.reshape(B, S, RW_ZCOLS), *consts)
    return y.reshape(B * S, W)


def _dsa_kernel(qlat_ref, qidx_ref, widx_ref, ckv_ref, ckvt_ref, kidx_ref, wuvt_ref, y_ref,
                  keys_ref, hi_ref, lo_ref, bias_ref, m_ref, acc_ref, *, k_sel):
    TQ, TK = DSA_TQ, DSA_TK
    HQ = 256
    SUBP = 16
    I16 = jnp.int16
    HALF = 1 << 15
    R = DSA_KV_RANK
    qi = pl.program_id(1)
    nkt = (qi + 1) * (TQ // TK)
    npair = nkt // 2
    keypos = lax.broadcasted_iota(I32, (TK, TQ), 0)
    rowpos = qi * TQ + lax.broadcasted_iota(I32, (TK, TQ), 1)
    one16 = jnp.ones((), BF16)
    zero16 = jnp.zeros((), BF16)

    def score_body(kt, carry):
        kx = kidx_ref[pl.ds(pl.multiple_of(kt * TK, TK), TK), :]
        sc = jnp.zeros((TK, TQ), F32)
        for h in range(IDX_HEADS):
            rel = jnp.maximum(_mm_nt(kx, qidx_ref[:, h * LANE:(h + 1) * LANE]), 0.0)
            sc = sc + widx_ref[h:h + 1, :] * rel
        sc = jnp.where(sc == 0.0, 0.0, sc)
        bits = pltpu.bitcast(sc, I32)
        key = jnp.where(bits >= 0, bits, bits ^ jnp.int32(0x7FFFFFFF))
        key = jnp.where(kt * TK + keypos <= rowpos, key, jnp.int32(INT_MIN))
        keys_ref[kt] = key
        hi_ref[kt] = (key >> 16).astype(I16)
        return carry

    lax.fori_loop(0, nkt, score_body, 0)

    def count16(ref, level):
        lv = jnp.broadcast_to(level, (SUBP, TQ)).astype(I16)

        def body(kp, acc):
            for j in range(2):
                for c in range(TK // SUBP):
                    kk = ref[2 * kp + j, c * SUBP:(c + 1) * SUBP, :]
                    acc = acc + jnp.where(kk >= lv, one16, zero16)
            return acc

        acc = lax.fori_loop(0, npair, body, jnp.zeros((SUBP, TQ), BF16))
        return jnp.sum(acc.astype(F32), axis=0, keepdims=True)

    def select16(ref, k_row):
        def bit_body(i, st):
            tu, n_above = st
            cand = tu | lax.shift_left(jnp.int32(1), 15 - i)
            cnt = count16(ref, cand - HALF)
            take = cnt >= k_row
            return jnp.where(take, cand, tu), jnp.where(take, n_above, cnt)
        return lax.fori_loop(0, 16, bit_body, (jnp.zeros((1, TQ), I32), jnp.zeros((1, TQ), F32)))

    th, n_gt_hi = select16(hi_ref, jnp.full((1, TQ), k_sel, F32))
    th = th - HALF

    def lo_body(kt, carry):
        key = keys_ref[kt]
        lo = (key & jnp.int32(0xFFFF)) - HALF
        lo_ref[kt] = jnp.where((key >> 16) == th, lo, -HALF).astype(I16)
        return carry

    lax.fori_loop(0, nkt, lo_body, 0)
    tl, n_gt_lo = select16(lo_ref, k_sel - n_gt_hi)
    n_gt = n_gt_hi + n_gt_lo
    thr = th * (1 << 16) + tl
    need = k_sel - n_gt

    m_ref[...] = jnp.full_like(m_ref, NEG)
    acc_ref[...] = jnp.zeros_like(acc_ref)
    lrow = lax.broadcasted_iota(I32, (TK + 8, TK), 0)
    lext = ((lax.broadcasted_iota(I32, (TK + 8, TK), 1) < lrow) | (lrow >= TK)).astype(BF16)

    def att_body(kp, eqc):
        for j in range(2):
            kt = 2 * kp + j
            key = keys_ref[kt]
            eq = key == thr
            pre = jnp.dot(lext, eq.astype(F32).astype(BF16), preferred_element_type=F32)
            sel = ((key > thr) | (eq & (pre[:TK] + eqc < need))) & (kt * TK + keypos <= rowpos)
            bias_ref[j * TK:(j + 1) * TK, :] = jnp.where(sel, 0.0, NEG)
            eqc = eqc + pre[TK:TK + 1]
        ck = ckv_ref[pl.ds(pl.multiple_of(kp * (2 * TK), 2 * TK), 2 * TK), :]
        ctt = ckvt_ref[kp]
        for h in range(DSA_HEADS):
            for r0 in range(0, TQ, HQ):
                s = _mm_nt(ck, qlat_ref[r0:r0 + HQ, h * R:(h + 1) * R]) + bias_ref[:, r0:r0 + HQ]
                m_old = m_ref[h, :, r0:r0 + HQ]
                m_new = jnp.maximum(m_old, jnp.max(s, axis=0, keepdims=True))
                p = jnp.exp2(s - m_new[0:1])
                acc_ref[h, :, r0:r0 + HQ] = (jnp.exp2(m_old - m_new)[0:1] * acc_ref[h, :, r0:r0 + HQ]
                                             + jnp.dot(ctt, p.astype(BF16), preferred_element_type=F32))
                m_ref[h, :, r0:r0 + HQ] = m_new
        return eqc

    lax.fori_loop(0, npair, att_body, jnp.zeros((1, TQ), F32))
    o = jnp.concatenate([(acc_ref[h, :R, :] / acc_ref[h, R:R + 1, :]).astype(BF16)
                         for h in range(DSA_HEADS)], axis=0)
    y_ref[...] = jnp.dot(wuvt_ref[...], o, preferred_element_type=F32).astype(y_ref.dtype)


def _dsa(qlat, qidx, widx, ckv, kidx, wuv_bd, B, S):
    TQ, TK = DSA_TQ, DSA_TK
    R = DSA_KV_RANK
    nq = S // TQ
    nk = S // TK
    assert TQ == 2 * TK and S % TQ == 0 and nk * TK // 16 <= 256
    k_sel = min(IDX_TOPK_MAX, S // 4)
    widx_t = widx[:, :8].T
    ckv_t = jnp.concatenate([ckv.reshape(B * nq, 2 * TK, R).transpose(0, 2, 1),
                             jnp.ones((B * nq, DSA_ONES, 2 * TK), ckv.dtype)], axis=1)
    qrow = lambda n: pl.BlockSpec((TQ, n), lambda b, q: (b * nq + q, 0))
    y_t = pl.pallas_call(
        functools.partial(_dsa_kernel, k_sel=k_sel),
        grid=(B, nq),
        in_specs=[qrow(DSA_HEADS * R), qrow(IDX_HEADS * LANE),
                  pl.BlockSpec((8, TQ), lambda b, q: (0, b * nq + q)),
                  pl.BlockSpec((S, R), lambda b, q: (b, 0)),
                  pl.BlockSpec((nq, R + DSA_ONES, 2 * TK), lambda b, q: (b, 0, 0)),
                  pl.BlockSpec((S, LANE), lambda b, q: (b, 0)),
                  _const_spec((DSA_WIDTH, DSA_HEADS * R))],
        out_specs=pl.BlockSpec((None, DSA_WIDTH, TQ), lambda b, q: (b, 0, q)),
        out_shape=jax.ShapeDtypeStruct((B, DSA_WIDTH, S), BF16),
        scratch_shapes=[pltpu.VMEM((nk, TK, TQ), I32), pltpu.VMEM((nk, TK, TQ), jnp.int16),
                        pltpu.VMEM((nk, TK, TQ), jnp.int16),
                        pltpu.VMEM((2 * TK, TQ), F32),
                        pltpu.VMEM((DSA_HEADS, 8, TQ), F32),
                        pltpu.VMEM((DSA_HEADS, R + DSA_ONES, TQ), F32)],
        compiler_params=_cparams(("parallel", "arbitrary")),
        name="dsa",
    )(qlat, qidx, widx_t, ckv, ckv_t, kidx, wuv_bd.T)
    return y_t.transpose(0, 2, 1).reshape(B * S, DSA_WIDTH)


def _memkv_kernel(m_ref, g_ref, w_ref, o_ref):
    o_ref[...] = _mm(_rms(m_ref[...], g_ref[...]), w_ref[...]).astype(o_ref.dtype)


def _memkv(mem2, norm_mem, w_ckv, M):
    R, D = mem2.shape
    return pl.pallas_call(
        _memkv_kernel,
        grid=(R // M,),
        in_specs=[pl.BlockSpec((M, D), lambda i: (i, 0)), _const_spec((1, D)), _const_spec(w_ckv.shape)],
        out_specs=pl.BlockSpec((M, w_ckv.shape[1]), lambda i: (i, 0)),
        out_shape=jax.ShapeDtypeStruct((R, w_ckv.shape[1]), BF16),
        compiler_params=_cparams(("parallel",)),
        name="memkv",
    )(mem2, norm_mem.reshape(1, D), w_ckv)


def _mix_kernel(x_ref, ya_ref, yb_ref, gate_ref, kv_ref, wpa_ref, wpb_ref, wout_ref, ncross_ref,
                wcq_ref, wco_ref, nffn_ref, wrh_ref, wrl_ref, br_ref,
                h2_ref, hn3_ref, ids_ref, gts_ref):
    D = x_ref.shape[1]
    xw = D // X_HEADS
    a = jnp.dot(ya_ref[...], wpa_ref[...], preferred_element_type=F32)
    b = jnp.dot(yb_ref[...], wpb_ref[...], preferred_element_type=F32)
    mix = gate_ref[:, :D] * a + gate_ref[:, D:] * b
    h1 = x_ref[...] + _mm(mix, wout_ref[...])
    q = _mm(_rms(h1, ncross_ref[...]), wcq_ref[...]).astype(BF16)
    outs = []
    for h in range(X_HEADS):
        s = _mm_nt(q[:, h * xw:(h + 1) * xw], kv_ref[:, h * xw:(h + 1) * xw]) * (xw ** -0.5)
        e = jnp.exp(s - jnp.max(s, axis=-1, keepdims=True))
        p = e / jnp.sum(e, axis=-1, keepdims=True)
        outs.append(_mm(p, kv_ref[:, D + h * xw:D + (h + 1) * xw]).astype(BF16))
    h2 = h1 + jnp.dot(jnp.concatenate(outs, axis=-1), wco_ref[...], preferred_element_type=F32)
    h2_ref[...] = h2
    hn3 = _rms(h2, nffn_ref[...])
    _store_row_tiles(hn3_ref, hn3)

    hi = hn3.astype(BF16)
    lo = (hn3 - hi.astype(F32)).astype(BF16)
    lg = (jnp.dot(hi, wrh_ref[...], preferred_element_type=F32)
          + jnp.dot(lo, wrh_ref[...], preferred_element_type=F32)
          + jnp.dot(hi, wrl_ref[...], preferred_element_type=F32)) + br_ref[...]
    lane = lax.broadcasted_iota(I32, lg.shape, 1)
    big = jnp.int32(1 << 20)
    rmax = lambda t: jnp.max(t, axis=-1, keepdims=True)
    rsum = lambda t: jnp.sum(t, axis=-1, keepdims=True)
    first = lambda hit: jnp.min(jnp.where(hit, lane, big), axis=-1, keepdims=True)
    gmask = lane < N_GROUPS
    gl = jnp.where(gmask, lg, NEG)
    ge = jnp.where(gmask, jnp.exp(gl - rmax(gl)), 0.0)
    gp = jnp.where(gmask, ge / rsum(ge), -1.0)
    p_grp = rmax(gp)
    grp = first(gp == p_grp)
    eidx = lane - N_GROUPS
    emask = (eidx >= 0) & (eidx < N_EXPERTS) & ((eidx // EXPERTS_PER_GROUP) == grp)
    el = jnp.where(emask, lg, NEG)
    ee = jnp.where(emask, jnp.exp(el - rmax(el)), 0.0)
    ep = jnp.where(emask, ee / rsum(ee), -1.0)
    p1 = rmax(ep)
    i1 = first(ep == p1)
    ep2 = jnp.where(lane == i1, -1.0, ep)
    p2 = rmax(ep2)
    i2 = first(ep2 == p2)
    den = p1 + p2
    ids_ref[...] = jnp.where(lane == 0, i1 - N_GROUPS, jnp.where(lane == 1, i2 - N_GROUPS, 0))
    gts_ref[...] = jnp.where(lane == 0, p_grp * p1 / den, jnp.where(lane == 1, p_grp * p2 / den, 0.0))


def _mix(x2, ya, yb, gates, kv, wpa, wpb, wout, ncross, wcq, wco, nffn, wr_hi, wr_lo, br, B, S, M, tm):
    T, D = x2.shape
    nt = S // tm
    row = lambda n: pl.BlockSpec((tm, n), lambda i: (i, 0))
    consts = [wpa, wpb, wout, ncross.reshape(1, D), wcq, wco, nffn.reshape(1, D), wr_hi, wr_lo, br]
    outs = [jax.ShapeDtypeStruct((T, D), F32), jax.ShapeDtypeStruct((T * (D // LANE), LANE), F32),
            jax.ShapeDtypeStruct((T, LANE), I32), jax.ShapeDtypeStruct((T, LANE), F32)]
    return pl.pallas_call(
        _mix_kernel,
        grid=(T // tm,),
        in_specs=[row(D), row(RW_WIDTH), row(DSA_WIDTH), row(2 * D),
                  pl.BlockSpec((M, 2 * D), lambda i: (i // nt, 0))]
                 + [_const_spec(t.shape) for t in consts],
        out_specs=[row(D), pl.BlockSpec((tm * (D // LANE), LANE), lambda i: (i, 0)), row(LANE), row(LANE)],
        out_shape=outs,
        compiler_params=_cparams(("parallel",)),
        name="mix_cross_router",
    )(x2, ya, yb, gates, kv, *consts)


def _expert_kernel(be_ref, nu_ref, tok_ref, ntok_ref, x_hbm, wg_ref, wu_ref, wd_ref, o_ref, xbuf_ref, sem):
    i = pl.program_id(0)
    n_used = nu_ref[0]
    cur = i % 2
    sub = wg_ref.shape[0] // LANE
    tile = lambda ref, r: ref.at[pl.ds(pl.multiple_of(r * sub, sub), sub), :]

    def gather(t_ref, b, unroll):
        def start(r, c):
            pltpu.make_async_copy(tile(x_hbm, t_ref[0, 0, r]), tile(xbuf_ref.at[b], r), sem.at[b]).start()
            return c
        lax.fori_loop(0, MOE_BLOCK, start, 0, unroll=unroll)

    def wait_block():
        pltpu.make_async_copy(x_hbm.at[pl.ds(0, MOE_BLOCK * sub), :], xbuf_ref.at[cur], sem.at[cur]).wait()

    @pl.when(i == 0)
    def _():
        gather(tok_ref, 0, 4)

    @pl.when(i < n_used)
    def _():
        wait_block()
        gather(ntok_ref, 1 - cur, True)
        x = _load_row_tiles(xbuf_ref.at[cur], MOE_BLOCK, sub).astype(BF16)
        gt = jnp.dot(x, wg_ref[...], preferred_element_type=F32)
        up = jnp.dot(x, wu_ref[...], preferred_element_type=F32)
        hid = gt * _sigmoid(gt) * up
        _store_row_tiles(o_ref, _mm(hid, wd_ref[...]))

    @pl.when(i >= n_used)
    def _():
        o_ref[...] = jnp.zeros_like(o_ref)

    @pl.when(i == n_used)
    def _():
        wait_block()


def _experts(hn, tok_of_slot, blk_expert, n_used, w_gate, w_up, w_down):
    E, D, F = w_gate.shape
    sub = D // LANE
    P = tok_of_slot.shape[0]
    nb = P // MOE_BLOCK
    tok3 = tok_of_slot.reshape(nb, 1, MOE_BLOCK)
    grid_spec = pltpu.PrefetchScalarGridSpec(
        num_scalar_prefetch=2,
        grid=(nb,),
        in_specs=[pl.BlockSpec((1, 1, MOE_BLOCK), lambda i, be, nu: (i, 0, 0), memory_space=pltpu.SMEM),
                  pl.BlockSpec((1, 1, MOE_BLOCK), lambda i, be, nu: (jnp.minimum(i + 1, nb - 1), 0, 0),
                               memory_space=pltpu.SMEM),
                  pl.BlockSpec(memory_space=pl.ANY),
                  pl.BlockSpec((None, D, F), lambda i, be, nu: (be[i], 0, 0)),
                  pl.BlockSpec((None, D, F), lambda i, be, nu: (be[i], 0, 0)),
                  pl.BlockSpec((None, F, D), lambda i, be, nu: (be[i], 0, 0))],
        out_specs=pl.BlockSpec((MOE_BLOCK * sub, LANE), lambda i, be, nu: (i, 0)),
        scratch_shapes=[pltpu.VMEM((2, MOE_BLOCK * sub, LANE), F32), pltpu.SemaphoreType.DMA((2,))],
    )
    return pl.pallas_call(
        _expert_kernel,
        grid_spec=grid_spec,
        out_shape=jax.ShapeDtypeStruct((P * sub, LANE), F32),
        compiler_params=_cparams(("arbitrary",)),
        name="experts",
    )(blk_expert, n_used, tok3, tok3, hn, w_gate, w_up, w_down)


def _final_kernel(slot_ref, nslot_ref, h_ref, gts_ref, g_ref, ys_ref, o_ref, buf_ref, sem, *, apply_norm):
    n_tok, d = h_ref.shape
    sub = d // LANE
    tile = lambda ref, r: ref.at[pl.ds(pl.multiple_of(r * sub, sub), sub), :]
    i = pl.program_id(0)
    cur = i % 2

    def gather(s_ref, b):
        def start(t, c):
            for j in range(2):
                pltpu.make_async_copy(tile(ys_ref, s_ref[0, 0, 2 * t + j]), tile(buf_ref.at[b, j], t),
                                      sem.at[b]).start()
            return c
        lax.fori_loop(0, n_tok, start, 0, unroll=4)

    @pl.when(i == 0)
    def _():
        gather(slot_ref, 0)

    @pl.when(i + 1 < pl.num_programs(0))
    def _():
        gather(nslot_ref, 1 - cur)

    for j in range(2):
        pltpu.make_async_copy(ys_ref.at[pl.ds(0, n_tok * sub), :], buf_ref.at[cur, j], sem.at[cur]).wait()
    y = (gts_ref[:, 0:1] * _load_row_tiles(buf_ref.at[cur, 0], n_tok, sub)
         + gts_ref[:, 1:2] * _load_row_tiles(buf_ref.at[cur, 1], n_tok, sub))
    h = h_ref[...] + y
    o_ref[...] = _rms(h, g_ref[...]) if apply_norm else h


def _final(h2, ys, slots, gts, norm_final, apply_norm):
    T, D = h2.shape
    tm = ROW_TOK
    nt = T // tm
    row = lambda n: pl.BlockSpec((tm, n), lambda i: (i, 0))
    slots3 = slots.reshape(nt, 1, 2 * tm)
    return pl.pallas_call(
        functools.partial(_final_kernel, apply_norm=apply_norm),
        grid=(nt,),
        in_specs=[pl.BlockSpec((1, 1, 2 * tm), lambda i: (i, 0, 0), memory_space=pltpu.SMEM),
                  pl.BlockSpec((1, 1, 2 * tm), lambda i: (jnp.minimum(i + 1, nt - 1), 0, 0),
                               memory_space=pltpu.SMEM),
                  row(D), row(LANE), _const_spec((1, D)), pl.BlockSpec(memory_space=pl.ANY)],
        out_specs=row(D),
        out_shape=jax.ShapeDtypeStruct((T, D), F32),
        scratch_shapes=[pltpu.VMEM((2, 2, tm * (D // LANE), LANE), F32), pltpu.SemaphoreType.DMA((2,))],
        compiler_params=_cparams(("arbitrary",)),
        name="final",
    )(slots3, slots3, h2, gts, norm_final.reshape(1, D), ys)


def _pad_cols(w, n):
    return jnp.pad(w, ((0, 0), (0, n - w.shape[1])))


def _pad_rows(w, n):
    return jnp.pad(w, ((0, n - w.shape[0]), (0, 0)))


def _layer(h, mem, norm_mix, w_in, shift_mu, rw_w0, rw_w2, rw_a0, rw_a2, rw_g2, rw_k_k, rw_k_a,
           rw_r_k, rw_ln_w, rw_ln_b, kv_norm, w_uk, w_uv, w_proj_a, w_proj_b, b_gate, w_out,
           norm_cross, norm_mem, w_cq, w_ckv, w_co, norm_ffn, w_router_g, b_router_g, w_router_e,
           b_router_e, w_e_gate, w_e_up, w_e_down):
    B, S, D = h.shape
    M = mem.shape[1]
    T = B * S
    W = RW_WIDTH
    x2 = h.reshape(T, D)

    c = 0
    pieces = {}
    for name, n in (("r", W), ("k", W), ("v", W), ("xw", RW_DECAY_LORA), ("xa", RW_AAA_LORA),
                    ("xg", RW_GATE_LORA), ("q", DSA_WIDTH), ("ckv", DSA_KV_RANK),
                    ("qidx", IDX_HEADS * IDX_DIM), ("kidx", IDX_DIM), ("widx", IDX_HEADS),
                    ("gate", 2 * D)):
        pieces[name] = (c, n)
        c += n
    col = lambda name: w_in[:, pieces[name][0]:pieces[name][0] + pieces[name][1]]
    mu_of = lambda name: shift_mu[pieces[name][0]:pieces[name][0] + pieces[name][1]]
    w_rw = jnp.concatenate([col("r"), col("k"), col("v"), _pad_cols(col("xw"), LANE),
                            _pad_cols(col("xa"), LANE), _pad_cols(col("xg"), LANE)], axis=1).astype(BF16)
    padv = lambda t: jnp.pad(t, (0, LANE - t.shape[0]))
    mu = jnp.concatenate([mu_of("r"), mu_of("k"), mu_of("v"), padv(mu_of("xw")), padv(mu_of("xa")),
                          padv(mu_of("xg"))])
    qidx_w = col("qidx").reshape(D, IDX_HEADS, IDX_DIM)
    qidx_w = jnp.pad(qidx_w, ((0, 0), (0, 0), (0, LANE - IDX_DIM))).reshape(D, IDX_HEADS * LANE)
    w_dsa = jnp.concatenate([col("q"), col("ckv"), qidx_w, _pad_cols(col("kidx"), LANE),
                             _pad_cols(col("widx"), LANE)], axis=1).astype(BF16)
    w_gate_in = col("gate").astype(BF16)
    eye_h = jnp.eye(DSA_HEADS, dtype=F32)
    wuk_bd = jnp.einsum("rhd,hg->hdgr", w_uk, eye_h).reshape(DSA_WIDTH, DSA_HEADS * DSA_KV_RANK).astype(BF16)
    wuv_bd = jnp.einsum("rhd,hg->hrgd", w_uv, eye_h).reshape(DSA_HEADS * DSA_KV_RANK, DSA_WIDTH).astype(BF16)

    tm1 = min(ROW_BLOCK, T)
    z_rw, qlat, ckv, qidx, kidx, widx, gates = _inproj(
        x2, norm_mix, w_rw, w_dsa, w_gate_in, b_gate, wuk_bd, kv_norm, tm1)

    y_a = _rwkv(z_rw, B, S, mu, rw_w0, _pad_rows(rw_w2, LANE).astype(BF16), rw_a0,
                _pad_rows(rw_a2, LANE).astype(BF16), rw_g2.astype(BF16), rw_k_k, rw_k_a, rw_r_k,
                rw_ln_w, rw_ln_b)
    y_b = _dsa(qlat, qidx, widx, ckv, kidx, wuv_bd, B, S)
    kv = _memkv(mem.reshape(B * M, D), norm_mem, w_ckv.astype(BF16), M)

    w_r = jnp.concatenate([w_router_g, w_router_e], axis=1)
    w_r = _pad_cols(w_r, LANE)
    wr_hi = w_r.astype(BF16)
    wr_lo = (w_r - wr_hi.astype(F32)).astype(BF16)
    b_r = jnp.pad(jnp.concatenate([b_router_g, b_router_e]), (0, LANE - N_GROUPS - N_EXPERTS)).reshape(1, LANE)
    tm5 = min(ROW_BLOCK, S)
    h2, hn3, ids, gts = _mix(x2, y_a, y_b, gates, kv, w_proj_a.astype(BF16), w_proj_b.astype(BF16),
                             w_out.astype(BF16), norm_cross, w_cq.astype(BF16), w_co.astype(BF16),
                             norm_ffn, wr_hi, wr_lo, b_r, B, S, M, tm5)

    A = 2 * T
    e_flat = ids[:, :2].reshape(A)
    onehot = (e_flat[:, None] == jnp.arange(N_EXPERTS, dtype=I32)[None, :]).astype(I32)
    csum = jnp.cumsum(onehot, axis=0)
    rank = jnp.sum(csum * onehot, axis=1) - 1
    counts = csum[-1]
    padded = (counts + MOE_BLOCK - 1) // MOE_BLOCK * MOE_BLOCK
    pend = jnp.cumsum(padded)
    pstart = pend - padded
    slots = (jnp.sum(pstart[None, :] * onehot, axis=1) + rank).astype(I32)
    P = A + N_EXPERTS * MOE_BLOCK
    nb = P // MOE_BLOCK
    blk_pos = jnp.arange(nb, dtype=I32) * MOE_BLOCK
    blk_expert = jnp.minimum(jnp.sum((pend[None, :] <= blk_pos[:, None]).astype(I32), axis=1),
                             N_EXPERTS - 1).astype(I32)
    n_used = (pend[-1:] // MOE_BLOCK).astype(I32)
    order = jnp.argsort(e_flat, stable=True).astype(I32)
    start = jnp.cumsum(counts) - counts
    slot_e = jnp.repeat(blk_expert, MOE_BLOCK)
    slot_r = jnp.arange(P, dtype=I32) - pstart[slot_e]
    slot_ok = slot_r < counts[slot_e]
    tok_of_slot = jnp.where(slot_ok, order[jnp.clip(start[slot_e] + slot_r, 0, A - 1)] // 2, 0).astype(I32)

    ys = _experts(hn3, tok_of_slot, blk_expert, n_used, w_e_gate.astype(BF16), w_e_up.astype(BF16),
                  w_e_down.astype(BF16))
    return h2, ys, slots, gts


def kernel(x, mem, norm_mix, w_in, shift_mu, rw_w0, rw_w2, rw_a0, rw_a2, rw_g2, rw_k_k, rw_k_a, rw_r_k, rw_ln_w, rw_ln_b, kv_norm, w_uk, w_uv, w_proj_a, w_proj_b, b_gate, w_out, norm_cross, norm_mem, w_cq, w_ckv, w_co, norm_ffn, w_router_g, b_router_g, w_router_e, b_router_e, w_e_gate, w_e_up, w_e_down, norm_final):
    B, S, D = x.shape
    depth = norm_mix.shape[0]
    h = x
    for l in range(depth):
        last = l == depth - 1
        h2, ys, slots, gts = _layer(
            h, mem, norm_mix[l], w_in[l], shift_mu[l], rw_w0[l], rw_w2[l], rw_a0[l], rw_a2[l],
            rw_g2[l], rw_k_k[l], rw_k_a[l], rw_r_k[l], rw_ln_w[l], rw_ln_b[l], kv_norm[l], w_uk[l],
            w_uv[l], w_proj_a[l], w_proj_b[l], b_gate[l], w_out[l], norm_cross[l], norm_mem[l],
            w_cq[l], w_ckv[l], w_co[l], norm_ffn[l], w_router_g[l], b_router_g[l], w_router_e[l],
            b_router_e[l], w_e_gate[l], w_e_up[l], w_e_down[l])
        h = _final(h2, ys, slots, gts, norm_final, last).reshape(B, S, D)
    return h
```

```python
import functools

import jax
import jax.numpy as jnp
from jax import lax
from jax.experimental import pallas as pl
from jax.experimental.pallas import tpu as pltpu

F32 = jnp.float32
BF16 = jnp.bfloat16
I32 = jnp.int32

RMS_EPS = 1e-6
RW_HEADS = 8
RW_HEAD_DIM = 64
RW_WIDTH = RW_HEADS * RW_HEAD_DIM
RW_DECAY_LORA = 64
RW_AAA_LORA = 64
RW_GATE_LORA = 128
RW_GN_EPS = 64e-5
DSA_HEADS = 8
DSA_HEAD_DIM = 64
DSA_WIDTH = DSA_HEADS * DSA_HEAD_DIM
DSA_KV_RANK = 128
IDX_HEADS = 4
IDX_DIM = 64
IDX_TOPK_MAX = 256
X_HEADS = 4
N_GROUPS = 4
EXPERTS_PER_GROUP = 8
N_EXPERTS = N_GROUPS * EXPERTS_PER_GROUP
MOE_BLOCK = 256

LANE = 128
RW_CHUNK = 64
RW_SEQS = 4
RW_ZCOLS = 3 * RW_WIDTH + 3 * LANE
DSA_ZCOLS = DSA_WIDTH + DSA_KV_RANK + IDX_HEADS * LANE + LANE + LANE
DSA_TQ = 512
DSA_TK = 256
DSA_ONES = 128
ROW_BLOCK = 512
ROW_TOK = 256
NEG = -1e30
LOG2E = 1.4426950408889634
INT_MIN = -(2 ** 31)
VMEM_LIMIT = 56 * 1024 * 1024


def _mm(a, b):
    return jnp.dot(a.astype(BF16), b.astype(BF16), preferred_element_type=F32)


def _mm_nt(a, b):
    return lax.dot_general(a.astype(BF16), b.astype(BF16), (((1,), (1,)), ((), ())),
                           preferred_element_type=F32)


def _mm_tn(a, b):
    return lax.dot_general(a.astype(BF16), b.astype(BF16), (((0,), (0,)), ((), ())),
                           preferred_element_type=F32)


def _split3(x):
    hi = x.astype(BF16)
    r1 = x - hi.astype(F32)
    mid = r1.astype(BF16)
    lo = (r1 - mid.astype(F32)).astype(BF16)
    return hi, mid, lo


def _mm_hilo_rhs(x, b):
    hi = x.astype(BF16)
    lo = (x - hi.astype(F32)).astype(BF16)
    return jnp.dot(hi, b, preferred_element_type=F32) + jnp.dot(lo, b, preferred_element_type=F32)


def _mm_exact_lhs(a, x):
    hi, mid, lo = _split3(x)
    return (jnp.dot(a, hi, preferred_element_type=F32) + jnp.dot(a, mid, preferred_element_type=F32)
            + jnp.dot(a, lo, preferred_element_type=F32))


def _rms(x, g):
    return x * lax.rsqrt(jnp.mean(x * x, axis=-1, keepdims=True) + RMS_EPS) * g


def _sigmoid(x):
    return 1.0 / (1.0 + jnp.exp(-x))


def _softplus(x):
    return jnp.maximum(x, 0.0) + jnp.log1p(jnp.exp(-jnp.abs(x)))


def _load_row_tiles(ref, rows, sub):
    return jnp.concatenate([ref[pl.ds(c, rows, stride=sub), :] for c in range(sub)], axis=-1)


def _store_row_tiles(ref, val):
    rows, d = val.shape
    sub = d // LANE
    for c in range(sub):
        ref[pl.ds(c, rows, stride=sub), :] = val[:, c * LANE:(c + 1) * LANE]


def _cparams(sem):
    return pltpu.CompilerParams(dimension_semantics=sem, vmem_limit_bytes=VMEM_LIMIT)


def _const_spec(shape):
    nd = len(shape)
    return pl.BlockSpec(shape, lambda *_: (0,) * nd, pipeline_mode=pl.Buffered(1))


def _inproj_kernel(x_ref, g_ref, wrw_ref, wdsa_ref, wgate_ref, bgate_ref, wuk_ref, kvn_ref,
                   zrw_ref, qlat_ref, ckv_ref, qidx_ref, kidx_ref, widx_ref, gate_ref):
    hn = _rms(x_ref[...], g_ref[...]).astype(BF16)
    zrw_ref[...] = jnp.dot(hn, wrw_ref[...], preferred_element_type=F32)
    zd = jnp.dot(hn, wdsa_ref[...], preferred_element_type=F32)
    o = 0
    q = zd[:, o:o + DSA_WIDTH]; o += DSA_WIDTH
    c = zd[:, o:o + DSA_KV_RANK]; o += DSA_KV_RANK
    qi = zd[:, o:o + IDX_HEADS * LANE]; o += IDX_HEADS * LANE
    ki = zd[:, o:o + LANE]; o += LANE
    wi = zd[:, o:o + LANE]
    qlat_ref[...] = (_mm(q, wuk_ref[...]) * (DSA_HEAD_DIM ** -0.5 * LOG2E)).astype(BF16)
    ckv_ref[...] = _rms(c, kvn_ref[...]).astype(BF16)
    qidx_ref[...] = qi.astype(BF16)
    kidx_ref[...] = ki.astype(BF16)
    widx_ref[...] = wi * ((IDX_HEADS * IDX_DIM) ** -0.5)
    zg = jnp.dot(hn, wgate_ref[...], preferred_element_type=F32) + bgate_ref[...]
    gate_ref[...] = _sigmoid(zg)


def _inproj(x2, norm_mix, w_rw, w_dsa, w_gate, b_gate, wuk_bd, kv_norm, tm):
    T, D = x2.shape
    grid = (T // tm,)
    row = lambda n: pl.BlockSpec((tm, n), lambda i: (i, 0))
    outs = [
        jax.ShapeDtypeStruct((T, RW_ZCOLS), F32),
        jax.ShapeDtypeStruct((T, DSA_HEADS * DSA_KV_RANK), BF16),
        jax.ShapeDtypeStruct((T, DSA_KV_RANK), BF16),
        jax.ShapeDtypeStruct((T, IDX_HEADS * LANE), BF16),
        jax.ShapeDtypeStruct((T, LANE), BF16),
        jax.ShapeDtypeStruct((T, LANE), F32),
        jax.ShapeDtypeStruct((T, 2 * D), F32),
    ]
    return pl.pallas_call(
        _inproj_kernel,
        grid=grid,
        in_specs=[row(D), _const_spec((1, D)), _const_spec(w_rw.shape), _const_spec(w_dsa.shape),
                  _const_spec(w_gate.shape), _const_spec((1, 2 * D)), _const_spec(wuk_bd.shape),
                  _const_spec((1, DSA_KV_RANK))],
        out_specs=[row(s.shape[1]) for s in outs],
        out_shape=outs,
        compiler_params=_cparams(("parallel",)),
        name="inproj",
    )(x2, norm_mix.reshape(1, D), w_rw, w_dsa, w_gate, b_gate.reshape(1, 2 * D), wuk_bd,
      kv_norm.reshape(1, DSA_KV_RANK))


def _rwkv_kernel(z_ref, mu_ref, w0_ref, w2_ref, a0_ref, a2_ref, g2_ref, kk_ref, ka_ref, rk_ref,
                 lnw_ref, lnb_ref, hs_ref, y_ref, state_ref, carry_ref, ybuf_ref):
    C, N, H = RW_CHUNK, RW_HEAD_DIM, RW_HEADS
    W = RW_WIDTH
    NB = z_ref.shape[0]

    @pl.when(pl.program_id(1) == 0)
    def _():
        state_ref[...] = jnp.zeros_like(state_ref)
        carry_ref[...] = jnp.zeros_like(carry_ref)

    hs = hs_ref[...]
    headsum = lambda t: _mm_hilo_rhs(t, hs)
    ti = lax.broadcasted_iota(I32, (C, C), 0)
    si = lax.broadcasted_iota(I32, (C, C), 1)
    ltri = (si <= ti).astype(BF16)
    t2 = lax.broadcasted_iota(I32, (C, 2 * C), 0)
    s2 = lax.broadcasted_iota(I32, (C, 2 * C), 1) % C
    m_strict = s2 < t2
    m_incl = s2 <= t2
    eye = (si == ti).astype(F32)
    zeros_cn = jnp.zeros((C, N), F32)
    row = lax.broadcasted_iota(I32, (C, RW_ZCOLS), 0)

    a_t, r_t, b_t, k_t, v, p_c, bonus, g = ([None] * NB for _ in range(8))
    for bi in range(NB):
        z = z_ref[bi]
        zprev = jnp.where(row == 0, carry_ref[bi], pltpu.roll(z, 1, 0))
        carry_ref[bi] = z[C - 1:C, :]
        zm = z + (zprev - z) * mu_ref[...]
        r = zm[:, 0:W]
        k = zm[:, W:2 * W]
        v[bi] = zm[:, 2 * W:3 * W]
        xw = zm[:, 3 * W:3 * W + LANE]
        xa = zm[:, 3 * W + LANE:3 * W + 2 * LANE]
        xg = zm[:, 3 * W + 2 * LANE:3 * W + 3 * LANE]
        wlog = -_softplus(-(w0_ref[...] + _mm(jnp.tanh(xw), w2_ref[...]))) - 0.5
        dlog = -jnp.exp(wlog)
        a = _sigmoid(a0_ref[...] + _mm(xa, a2_ref[...]))
        g[bi] = _mm(_sigmoid(xg), g2_ref[...])
        kk = k * kk_ref[...]
        kk = kk / jnp.maximum(jnp.sqrt(headsum(kk * kk)), 1e-12)
        k_eff = k * (1.0 + (a - 1.0) * ka_ref[...])
        bonus[bi] = headsum(r * k_eff * rk_ref[...]) * v[bi]
        cum = _mm_exact_lhs(ltri, dlog)
        pin = jnp.exp(cum)
        pinv = jnp.exp(-cum)
        a_t[bi] = -kk * jnp.exp(cum - dlog)
        r_t[bi] = r * pin
        b_t[bi] = kk * a * pinv
        k_t[bi] = k_eff * pinv
        p_c[bi] = pin[C - 1:C, :]

    units = [(bi, h) for bi in range(NB) for h in range(H)]
    ur = range(len(units))
    sl = lambda h: slice(h * N, (h + 1) * N)
    ar = [jnp.concatenate([a_t[bi][:, sl(h)], r_t[bi][:, sl(h)]], axis=0).astype(BF16) for bi, h in units]
    bk = [jnp.concatenate([b_t[bi][:, sl(h)], k_t[bi][:, sl(h)]], axis=0).astype(BF16) for bi, h in units]
    vh = [v[bi][:, sl(h)] for bi, h in units]
    s0 = [state_ref[bi, h] for bi, h in units]
    q = [_mm_nt(ar[u], bk[u]) for u in ur]
    w_all = [_mm_nt(ar[u], s0[u]) for u in ur]
    top = [jnp.where(m_strict, q[u][:C], 0.0).astype(BF16) for u in ur]
    bot = [jnp.where(m_incl, q[u][C:], 0.0).astype(BF16) for u in ur]
    rhs = [w_all[u][:C] + _mm(top[u], jnp.concatenate([zeros_cn, vh[u]], axis=0)) for u in ur]
    pw = [top[u][:, :C] for u in ur]
    inv = [eye + pw[u].astype(F32) for u in ur]
    for _ in range(5):
        pw = [_mm(pw[u], pw[u]).astype(BF16) for u in ur]
        inv = [inv[u] + _mm(pw[u], inv[u]) for u in ur]
    uu = [_mm(inv[u], rhs[u]) for u in ur]
    gsrc = [jnp.concatenate([uu[u], vh[u]], axis=0).astype(BF16) for u in ur]
    for u, (bi, h) in enumerate(units):
        ybuf_ref[bi, :, sl(h)] = w_all[u][C:] + _mm(bot[u], gsrc[u])
    for u, (bi, h) in enumerate(units):
        state_ref[bi, h] = (s0[u] + _mm_tn(gsrc[u], bk[u])) * p_c[bi][:, sl(h)]

    for bi in range(NB):
        y = ybuf_ref[bi]
        mean = headsum(y) * (1.0 / N)
        d = y - mean
        var = headsum(d * d) * (1.0 / N)
        yn = d * lax.rsqrt(var + RW_GN_EPS) * lnw_ref[...] + lnb_ref[...]
        y_ref[bi] = ((yn + bonus[bi]) * g[bi]).astype(y_ref.dtype)


def _rwkv(z_rw, B, S, mu, w0, w2, a0, a2, g2, k_k, k_a, r_k, ln_w, ln_b):
    C, W = RW_CHUNK, RW_WIDTH
    nb = RW_SEQS if B % RW_SEQS == 0 else 1
    nc = S // C
    hid = jnp.arange(W) // RW_HEAD_DIM
    hs = (hid[:, None] == hid[None, :]).astype(BF16)
    vec = lambda t: t.reshape(1, -1)
    consts = [vec(mu), vec(w0), w2, vec(a0), a2, g2, vec(k_k), vec(k_a), vec(r_k), vec(ln_w),
              vec(ln_b), hs]
    y = pl.pallas_call(
        _rwkv_kernel,
        grid=(B // nb, nc),
        in_specs=[pl.BlockSpec((nb, C, RW_ZCOLS), lambda b, c: (b, c, 0))]
                 + [_const_spec(t.shape) for t in consts],
        out_specs=pl.BlockSpec((nb, C, W), lambda b, c: (b, c, 0)),
        out_shape=jax.ShapeDtypeStruct((B, S, W), BF16),
        scratch_shapes=[pltpu.VMEM((nb, RW_HEADS, RW_HEAD_DIM, RW_HEAD_DIM), F32),
                        pltpu.VMEM((nb, 1, RW_ZCOLS), F32),
                        pltpu.VMEM((nb, C, W), F32)],
        compiler_params=_cparams(("parallel", "arbitrary")),
        name="rwkv",
    )(z_rw.reshape(B, S, RW_ZCOLS), *consts)
    return y.reshape(B * S, W)


def _dsa_kernel(qlat_ref, qidx_ref, widx_ref, ckv_ref, ckvt_ref, kidx_ref, wuvt_ref, y_ref,
                  keys_ref, hi_ref, lo_ref, bias_ref, m_ref, acc_ref, *, k_sel):
    TQ, TK = DSA_TQ, DSA_TK
    HQ = 256
    SUBP = 16
    I16 = jnp.int16
    HALF = 1 << 15
    R = DSA_KV_RANK
    qi = pl.program_id(1)
    nkt = (qi + 1) * (TQ // TK)
    npair = nkt // 2
    keypos = lax.broadcasted_iota(I32, (TK, TQ), 0)
    rowpos = qi * TQ + lax.broadcasted_iota(I32, (TK, TQ), 1)
    one16 = jnp.ones((), BF16)
    zero16 = jnp.zeros((), BF16)

    def score_body(kt, carry):
        kx = kidx_ref[pl.ds(pl.multiple_of(kt * TK, TK), TK), :]
        sc = jnp.zeros((TK, TQ), F32)
        for h in range(IDX_HEADS):
            rel = jnp.maximum(_mm_nt(kx, qidx_ref[:, h * LANE:(h + 1) * LANE]), 0.0)
            sc = sc + widx_ref[h:h + 1, :] * rel
        sc = jnp.where(sc == 0.0, 0.0, sc)
        bits = pltpu.bitcast(sc, I32)
        key = jnp.where(bits >= 0, bits, bits ^ jnp.int32(0x7FFFFFFF))
        key = jnp.where(kt * TK + keypos <= rowpos, key, jnp.int32(INT_MIN))
        keys_ref[kt] = key
        hi_ref[kt] = (key >> 16).astype(I16)
        return carry

    lax.fori_loop(0, nkt, score_body, 0)

    def count16(ref, level):
        lv = jnp.broadcast_to(level, (SUBP, TQ)).astype(I16)

        def body(kp, acc):
            for j in range(2):
                for c in range(TK // SUBP):
                    kk = ref[2 * kp + j, c * SUBP:(c + 1) * SUBP, :]
                    acc = acc + jnp.where(kk >= lv, one16, zero16)
            return acc

        acc = lax.fori_loop(0, npair, body, jnp.zeros((SUBP, TQ), BF16))
        return jnp.sum(acc.astype(F32), axis=0, keepdims=True)

    def select16(ref, k_row):
        def bit_body(i, st):
            tu, n_above = st
            cand = tu | lax.shift_left(jnp.int32(1), 15 - i)
            cnt = count16(ref, cand - HALF)
            take = cnt >= k_row
            return jnp.where(take, cand, tu), jnp.where(take, n_above, cnt)
        return lax.fori_loop(0, 16, bit_body, (jnp.zeros((1, TQ), I32), jnp.zeros((1, TQ), F32)))

    th, n_gt_hi = select16(hi_ref, jnp.full((1, TQ), k_sel, F32))
    th = th - HALF

    def lo_body(kt, carry):
        key = keys_ref[kt]
        lo = (key & jnp.int32(0xFFFF)) - HALF
        lo_ref[kt] = jnp.where((key >> 16) == th, lo, -HALF).astype(I16)
        return carry

    lax.fori_loop(0, nkt, lo_body, 0)
    tl, n_gt_lo = select16(lo_ref, k_sel - n_gt_hi)
    n_gt = n_gt_hi + n_gt_lo
    thr = th * (1 << 16) + tl
    need = k_sel - n_gt

    m_ref[...] = jnp.full_like(m_ref, NEG)
    acc_ref[...] = jnp.zeros_like(acc_ref)
    lrow = lax.broadcasted_iota(I32, (TK + 8, TK), 0)
    lext = ((lax.broadcasted_iota(I32, (TK + 8, TK), 1) < lrow) | (lrow >= TK)).astype(BF16)

    def att_body(kp, eqc):
        for j in range(2):
            kt = 2 * kp + j
            key = keys_ref[kt]
            eq = key == thr
            pre = jnp.dot(lext, eq.astype(F32).astype(BF16), preferred_element_type=F32)
            sel = ((key > thr) | (eq & (pre[:TK] + eqc < need))) & (kt * TK + keypos <= rowpos)
            bias_ref[j * TK:(j + 1) * TK, :] = jnp.where(sel, 0.0, NEG)
            eqc = eqc + pre[TK:TK + 1]
        ck = ckv_ref[pl.ds(pl.multiple_of(kp * (2 * TK), 2 * TK), 2 * TK), :]
        ctt = ckvt_ref[kp]
        for h in range(DSA_HEADS):
            for r0 in range(0, TQ, HQ):
                s = _mm_nt(ck, qlat_ref[r0:r0 + HQ, h * R:(h + 1) * R]) + bias_ref[:, r0:r0 + HQ]
                m_old = m_ref[h, :, r0:r0 + HQ]
                m_new = jnp.maximum(m_old, jnp.max(s, axis=0, keepdims=True))
                p = jnp.exp2(s - m_new[0:1])
                acc_ref[h, :, r0:r0 + HQ] = (jnp.exp2(m_old - m_new)[0:1] * acc_ref[h, :, r0:r0 + HQ]
                                             + jnp.dot(ctt, p.astype(BF16), preferred_element_type=F32))
                m_ref[h, :, r0:r0 + HQ] = m_new
        return eqc

    lax.fori_loop(0, npair, att_body, jnp.zeros((1, TQ), F32))
    o = jnp.concatenate([(acc_ref[h, :R, :] / acc_ref[h, R:R + 1, :]).astype(BF16)
                         for h in range(DSA_HEADS)], axis=0)
    y_ref[...] = jnp.dot(wuvt_ref[...], o, preferred_element_type=F32).astype(y_ref.dtype)


def _dsa(qlat, qidx, widx, ckv, kidx, wuv_bd, B, S):
    TQ, TK = DSA_TQ, DSA_TK
    R = DSA_KV_RANK
    nq = S // TQ
    nk = S // TK
    assert TQ == 2 * TK and S % TQ == 0 and nk * TK // 16 <= 256
    k_sel = min(IDX_TOPK_MAX, S // 4)
    widx_t = widx[:, :8].T
    ckv_t = jnp.concatenate([ckv.reshape(B * nq, 2 * TK, R).transpose(0, 2, 1),
                             jnp.ones((B * nq, DSA_ONES, 2 * TK), ckv.dtype)], axis=1)
    qrow = lambda n: pl.BlockSpec((TQ, n), lambda b, q: (b * nq + q, 0))
    y_t = pl.pallas_call(
        functools.partial(_dsa_kernel, k_sel=k_sel),
        grid=(B, nq),
        in_specs=[qrow(DSA_HEADS * R), qrow(IDX_HEADS * LANE),
                  pl.BlockSpec((8, TQ), lambda b, q: (0, b * nq + q)),
                  pl.BlockSpec((S, R), lambda b, q: (b, 0)),
                  pl.BlockSpec((nq, R + DSA_ONES, 2 * TK), lambda b, q: (b, 0, 0)),
                  pl.BlockSpec((S, LANE), lambda b, q: (b, 0)),
                  _const_spec((DSA_WIDTH, DSA_HEADS * R))],
        out_specs=pl.BlockSpec((None, DSA_WIDTH, TQ), lambda b, q: (b, 0, q)),
        out_shape=jax.ShapeDtypeStruct((B, DSA_WIDTH, S), BF16),
        scratch_shapes=[pltpu.VMEM((nk, TK, TQ), I32), pltpu.VMEM((nk, TK, TQ), jnp.int16),
                        pltpu.VMEM((nk, TK, TQ), jnp.int16),
                        pltpu.VMEM((2 * TK, TQ), F32),
                        pltpu.VMEM((DSA_HEADS, 8, TQ), F32),
                        pltpu.VMEM((DSA_HEADS, R + DSA_ONES, TQ), F32)],
        compiler_params=_cparams(("parallel", "arbitrary")),
        name="dsa",
    )(qlat, qidx, widx_t, ckv, ckv_t, kidx, wuv_bd.T)
    return y_t.transpose(0, 2, 1).reshape(B * S, DSA_WIDTH)


def _memkv_kernel(m_ref, g_ref, w_ref, o_ref):
    o_ref[...] = _mm(_rms(m_ref[...], g_ref[...]), w_ref[...]).astype(o_ref.dtype)


def _memkv(mem2, norm_mem, w_ckv, M):
    R, D = mem2.shape
    return pl.pallas_call(
        _memkv_kernel,
        grid=(R // M,),
        in_specs=[pl.BlockSpec((M, D), lambda i: (i, 0)), _const_spec((1, D)), _const_spec(w_ckv.shape)],
        out_specs=pl.BlockSpec((M, w_ckv.shape[1]), lambda i: (i, 0)),
        out_shape=jax.ShapeDtypeStruct((R, w_ckv.shape[1]), BF16),
        compiler_params=_cparams(("parallel",)),
        name="memkv",
    )(mem2, norm_mem.reshape(1, D), w_ckv)


def _mix_kernel(x_ref, ya_ref, yb_ref, gate_ref, kv_ref, wpa_ref, wpb_ref, wout_ref, ncross_ref,
                wcq_ref, wco_ref, nffn_ref, wrh_ref, wrl_ref, br_ref,
                h2_ref, hn3_ref, ids_ref, gts_ref):
    D = x_ref.shape[1]
    xw = D // X_HEADS
    a = jnp.dot(ya_ref[...], wpa_ref[...], preferred_element_type=F32)
    b = jnp.dot(yb_ref[...], wpb_ref[...], preferred_element_type=F32)
    mix = gate_ref[:, :D] * a + gate_ref[:, D:] * b
    h1 = x_ref[...] + _mm(mix, wout_ref[...])
    q = _mm(_rms(h1, ncross_ref[...]), wcq_ref[...]).astype(BF16)
    outs = []
    for h in range(X_HEADS):
        s = _mm_nt(q[:, h * xw:(h + 1) * xw], kv_ref[:, h * xw:(h + 1) * xw]) * (xw ** -0.5)
        e = jnp.exp(s - jnp.max(s, axis=-1, keepdims=True))
        p = e / jnp.sum(e, axis=-1, keepdims=True)
        outs.append(_mm(p, kv_ref[:, D + h * xw:D + (h + 1) * xw]).astype(BF16))
    h2 = h1 + jnp.dot(jnp.concatenate(outs, axis=-1), wco_ref[...], preferred_element_type=F32)
    h2_ref[...] = h2
    hn3 = _rms(h2, nffn_ref[...])
    _store_row_tiles(hn3_ref, hn3)

    hi = hn3.astype(BF16)
    lo = (hn3 - hi.astype(F32)).astype(BF16)
    lg = (jnp.dot(hi, wrh_ref[...], preferred_element_type=F32)
          + jnp.dot(lo, wrh_ref[...], preferred_element_type=F32)
          + jnp.dot(hi, wrl_ref[...], preferred_element_type=F32)) + br_ref[...]
    lane = lax.broadcasted_iota(I32, lg.shape, 1)
    big = jnp.int32(1 << 20)
    rmax = lambda t: jnp.max(t, axis=-1, keepdims=True)
    rsum = lambda t: jnp.sum(t, axis=-1, keepdims=True)
    first = lambda hit: jnp.min(jnp.where(hit, lane, big), axis=-1, keepdims=True)
    gmask = lane < N_GROUPS
    gl = jnp.where(gmask, lg, NEG)
    ge = jnp.where(gmask, jnp.exp(gl - rmax(gl)), 0.0)
    gp = jnp.where(gmask, ge / rsum(ge), -1.0)
    p_grp = rmax(gp)
    grp = first(gp == p_grp)
    eidx = lane - N_GROUPS
    emask = (eidx >= 0) & (eidx < N_EXPERTS) & ((eidx // EXPERTS_PER_GROUP) == grp)
    el = jnp.where(emask, lg, NEG)
    ee = jnp.where(emask, jnp.exp(el - rmax(el)), 0.0)
    ep = jnp.where(emask, ee / rsum(ee), -1.0)
    p1 = rmax(ep)
    i1 = first(ep == p1)
    ep2 = jnp.where(lane == i1, -1.0, ep)
    p2 = rmax(ep2)
    i2 = first(ep2 == p2)
    den = p1 + p2
    ids_ref[...] = jnp.where(lane == 0, i1 - N_GROUPS, jnp.where(lane == 1, i2 - N_GROUPS, 0))
    gts_ref[...] = jnp.where(lane == 0, p_grp * p1 / den, jnp.where(lane == 1, p_grp * p2 / den, 0.0))


def _mix(x2, ya, yb, gates, kv, wpa, wpb, wout, ncross, wcq, wco, nffn, wr_hi, wr_lo, br, B, S, M, tm):
    T, D = x2.shape
    nt = S // tm
    row = lambda n: pl.BlockSpec((tm, n), lambda i: (i, 0))
    consts = [wpa, wpb, wout, ncross.reshape(1, D), wcq, wco, nffn.reshape(1, D), wr_hi, wr_lo, br]
    outs = [jax.ShapeDtypeStruct((T, D), F32), jax.ShapeDtypeStruct((T * (D // LANE), LANE), F32),
            jax.ShapeDtypeStruct((T, LANE), I32), jax.ShapeDtypeStruct((T, LANE), F32)]
    return pl.pallas_call(
        _mix_kernel,
        grid=(T // tm,),
        in_specs=[row(D), row(RW_WIDTH), row(DSA_WIDTH), row(2 * D),
                  pl.BlockSpec((M, 2 * D), lambda i: (i // nt, 0))]
                 + [_const_spec(t.shape) for t in consts],
        out_specs=[row(D), pl.BlockSpec((tm * (D // LANE), LANE), lambda i: (i, 0)), row(LANE), row(LANE)],
        out_shape=outs,
        compiler_params=_cparams(("parallel",)),
        name="mix_cross_router",
    )(x2, ya, yb, gates, kv, *consts)


def _expert_kernel(be_ref, nu_ref, tok_ref, ntok_ref, x_hbm, wg_ref, wu_ref, wd_ref, o_ref, xbuf_ref, sem):
    i = pl.program_id(0)
    n_used = nu_ref[0]
    cur = i % 2
    sub = wg_ref.shape[0] // LANE
    tile = lambda ref, r: ref.at[pl.ds(pl.multiple_of(r * sub, sub), sub), :]

    def gather(t_ref, b, unroll):
        def start(r, c):
            pltpu.make_async_copy(tile(x_hbm, t_ref[0, 0, r]), tile(xbuf_ref.at[b], r), sem.at[b]).start()
            return c
        lax.fori_loop(0, MOE_BLOCK, start, 0, unroll=unroll)

    def wait_block():
        pltpu.make_async_copy(x_hbm.at[pl.ds(0, MOE_BLOCK * sub), :], xbuf_ref.at[cur], sem.at[cur]).wait()

    @pl.when(i == 0)
    def _():
        gather(tok_ref, 0, 4)

    @pl.when(i < n_used)
    def _():
        wait_block()
        gather(ntok_ref, 1 - cur, True)
        x = _load_row_tiles(xbuf_ref.at[cur], MOE_BLOCK, sub).astype(BF16)
        gt = jnp.dot(x, wg_ref[...], preferred_element_type=F32)
        up = jnp.dot(x, wu_ref[...], preferred_element_type=F32)
        hid = gt * _sigmoid(gt) * up
        _store_row_tiles(o_ref, _mm(hid, wd_ref[...]))

    @pl.when(i >= n_used)
    def _():
        o_ref[...] = jnp.zeros_like(o_ref)

    @pl.when(i == n_used)
    def _():
        wait_block()


def _experts(hn, tok_of_slot, blk_expert, n_used, w_gate, w_up, w_down):
    E, D, F = w_gate.shape
    sub = D // LANE
    P = tok_of_slot.shape[0]
    nb = P // MOE_BLOCK
    tok3 = tok_of_slot.reshape(nb, 1, MOE_BLOCK)
    grid_spec = pltpu.PrefetchScalarGridSpec(
        num_scalar_prefetch=2,
        grid=(nb,),
        in_specs=[pl.BlockSpec((1, 1, MOE_BLOCK), lambda i, be, nu: (i, 0, 0), memory_space=pltpu.SMEM),
                  pl.BlockSpec((1, 1, MOE_BLOCK), lambda i, be, nu: (jnp.minimum(i + 1, nb - 1), 0, 0),
                               memory_space=pltpu.SMEM),
                  pl.BlockSpec(memory_space=pl.ANY),
                  pl.BlockSpec((None, D, F), lambda i, be, nu: (be[i], 0, 0)),
                  pl.BlockSpec((None, D, F), lambda i, be, nu: (be[i], 0, 0)),
                  pl.BlockSpec((None, F, D), lambda i, be, nu: (be[i], 0, 0))],
        out_specs=pl.BlockSpec((MOE_BLOCK * sub, LANE), lambda i, be, nu: (i, 0)),
        scratch_shapes=[pltpu.VMEM((2, MOE_BLOCK * sub, LANE), F32), pltpu.SemaphoreType.DMA((2,))],
    )
    return pl.pallas_call(
        _expert_kernel,
        grid_spec=grid_spec,
        out_shape=jax.ShapeDtypeStruct((P * sub, LANE), F32),
        compiler_params=_cparams(("arbitrary",)),
        name="experts",
    )(blk_expert, n_used, tok3, tok3, hn, w_gate, w_up, w_down)


def _final_kernel(slot_ref, nslot_ref, h_ref, gts_ref, g_ref, ys_ref, o_ref, buf_ref, sem, *, apply_norm):
    n_tok, d = h_ref.shape
    sub = d // LANE
    tile = lambda ref, r: ref.at[pl.ds(pl.multiple_of(r * sub, sub), sub), :]
    i = pl.program_id(0)
    cur = i % 2

    def gather(s_ref, b, unroll):
        def start(t, c):
            for j in range(2):
                pltpu.make_async_copy(tile(ys_ref, s_ref[0, 0, 2 * t + j]), tile(buf_ref.at[b, j], t),
                                      sem.at[b]).start()
            return c
        lax.fori_loop(0, n_tok, start, 0, unroll=unroll)

    def drain(b):
        for j in range(2):
            pltpu.make_async_copy(ys_ref.at[pl.ds(0, n_tok * sub), :], buf_ref.at[b, j], sem.at[b]).wait()

    @pl.when(i == 0)
    def _():
        gather(slot_ref, 0, 4)

    gather(nslot_ref, 1 - cur, True)
    drain(cur)
    y = (gts_ref[:, 0:1] * _load_row_tiles(buf_ref.at[cur, 0], n_tok, sub)
         + gts_ref[:, 1:2] * _load_row_tiles(buf_ref.at[cur, 1], n_tok, sub))
    h = h_ref[...] + y
    o_ref[...] = _rms(h, g_ref[...]) if apply_norm else h

    @pl.when(i + 1 == pl.num_programs(0))
    def _():
        drain(1 - cur)


def _final(h2, ys, slots, gts, norm_final, apply_norm):
    T, D = h2.shape
    tm = ROW_TOK
    nt = T // tm
    row = lambda n: pl.BlockSpec((tm, n), lambda i: (i, 0))
    slots3 = slots.reshape(nt, 1, 2 * tm)
    return pl.pallas_call(
        functools.partial(_final_kernel, apply_norm=apply_norm),
        grid=(nt,),
        in_specs=[pl.BlockSpec((1, 1, 2 * tm), lambda i: (i, 0, 0), memory_space=pltpu.SMEM),
                  pl.BlockSpec((1, 1, 2 * tm), lambda i: (jnp.minimum(i + 1, nt - 1), 0, 0),
                               memory_space=pltpu.SMEM),
                  row(D), row(LANE), _const_spec((1, D)), pl.BlockSpec(memory_space=pl.ANY)],
        out_specs=row(D),
        out_shape=jax.ShapeDtypeStruct((T, D), F32),
        scratch_shapes=[pltpu.VMEM((2, 2, tm * (D // LANE), LANE), F32), pltpu.SemaphoreType.DMA((2,))],
        compiler_params=_cparams(("arbitrary",)),
        name="final",
    )(slots3, slots3, h2, gts, norm_final.reshape(1, D), ys)


def _pad_cols(w, n):
    return jnp.pad(w, ((0, 0), (0, n - w.shape[1])))


def _pad_rows(w, n):
    return jnp.pad(w, ((0, n - w.shape[0]), (0, 0)))


def _layer(h, mem, norm_mix, w_in, shift_mu, rw_w0, rw_w2, rw_a0, rw_a2, rw_g2, rw_k_k, rw_k_a,
           rw_r_k, rw_ln_w, rw_ln_b, kv_norm, w_uk, w_uv, w_proj_a, w_proj_b, b_gate, w_out,
           norm_cross, norm_mem, w_cq, w_ckv, w_co, norm_ffn, w_router_g, b_router_g, w_router_e,
           b_router_e, w_e_gate, w_e_up, w_e_down):
    B, S, D = h.shape
    M = mem.shape[1]
    T = B * S
    W = RW_WIDTH
    x2 = h.reshape(T, D)

    c = 0
    pieces = {}
    for name, n in (("r", W), ("k", W), ("v", W), ("xw", RW_DECAY_LORA), ("xa", RW_AAA_LORA),
                    ("xg", RW_GATE_LORA), ("q", DSA_WIDTH), ("ckv", DSA_KV_RANK),
                    ("qidx", IDX_HEADS * IDX_DIM), ("kidx", IDX_DIM), ("widx", IDX_HEADS),
                    ("gate", 2 * D)):
        pieces[name] = (c, n)
        c += n
    col = lambda name: w_in[:, pieces[name][0]:pieces[name][0] + pieces[name][1]]
    mu_of = lambda name: shift_mu[pieces[name][0]:pieces[name][0] + pieces[name][1]]
    w_rw = jnp.concatenate([col("r"), col("k"), col("v"), _pad_cols(col("xw"), LANE),
                            _pad_cols(col("xa"), LANE), _pad_cols(col("xg"), LANE)], axis=1).astype(BF16)
    padv = lambda t: jnp.pad(t, (0, LANE - t.shape[0]))
    mu = jnp.concatenate([mu_of("r"), mu_of("k"), mu_of("v"), padv(mu_of("xw")), padv(mu_of("xa")),
                          padv(mu_of("xg"))])
    qidx_w = col("qidx").reshape(D, IDX_HEADS, IDX_DIM)
    qidx_w = jnp.pad(qidx_w, ((0, 0), (0, 0), (0, LANE - IDX_DIM))).reshape(D, IDX_HEADS * LANE)
    w_dsa = jnp.concatenate([col("q"), col("ckv"), qidx_w, _pad_cols(col("kidx"), LANE),
                             _pad_cols(col("widx"), LANE)], axis=1).astype(BF16)
    w_gate_in = col("gate").astype(BF16)
    eye_h = jnp.eye(DSA_HEADS, dtype=F32)
    wuk_bd = jnp.einsum("rhd,hg->hdgr", w_uk, eye_h).reshape(DSA_WIDTH, DSA_HEADS * DSA_KV_RANK).astype(BF16)
    wuv_bd = jnp.einsum("rhd,hg->hrgd", w_uv, eye_h).reshape(DSA_HEADS * DSA_KV_RANK, DSA_WIDTH).astype(BF16)

    tm1 = min(ROW_BLOCK, T)
    z_rw, qlat, ckv, qidx, kidx, widx, gates = _inproj(
        x2, norm_mix, w_rw, w_dsa, w_gate_in, b_gate, wuk_bd, kv_norm, tm1)

    y_a = _rwkv(z_rw, B, S, mu, rw_w0, _pad_rows(rw_w2, LANE).astype(BF16), rw_a0,
                _pad_rows(rw_a2, LANE).astype(BF16), rw_g2.astype(BF16), rw_k_k, rw_k_a, rw_r_k,
                rw_ln_w, rw_ln_b)
    y_b = _dsa(qlat, qidx, widx, ckv, kidx, wuv_bd, B, S)
    kv = _memkv(mem.reshape(B * M, D), norm_mem, w_ckv.astype(BF16), M)

    w_r = jnp.concatenate([w_router_g, w_router_e], axis=1)
    w_r = _pad_cols(w_r, LANE)
    wr_hi = w_r.astype(BF16)
    wr_lo = (w_r - wr_hi.astype(F32)).astype(BF16)
    b_r = jnp.pad(jnp.concatenate([b_router_g, b_router_e]), (0, LANE - N_GROUPS - N_EXPERTS)).reshape(1, LANE)
    tm5 = min(ROW_BLOCK, S)
    h2, hn3, ids, gts = _mix(x2, y_a, y_b, gates, kv, w_proj_a.astype(BF16), w_proj_b.astype(BF16),
                             w_out.astype(BF16), norm_cross, w_cq.astype(BF16), w_co.astype(BF16),
                             norm_ffn, wr_hi, wr_lo, b_r, B, S, M, tm5)

    A = 2 * T
    e_flat = ids[:, :2].reshape(A)
    onehot = (e_flat[:, None] == jnp.arange(N_EXPERTS, dtype=I32)[None, :]).astype(I32)
    csum = jnp.cumsum(onehot, axis=0)
    rank = jnp.sum(csum * onehot, axis=1) - 1
    counts = csum[-1]
    padded = (counts + MOE_BLOCK - 1) // MOE_BLOCK * MOE_BLOCK
    pend = jnp.cumsum(padded)
    pstart = pend - padded
    slots = (jnp.sum(pstart[None, :] * onehot, axis=1) + rank).astype(I32)
    P = A + N_EXPERTS * MOE_BLOCK
    nb = P // MOE_BLOCK
    blk_pos = jnp.arange(nb, dtype=I32) * MOE_BLOCK
    blk_expert = jnp.minimum(jnp.sum((pend[None, :] <= blk_pos[:, None]).astype(I32), axis=1),
                             N_EXPERTS - 1).astype(I32)
    n_used = (pend[-1:] // MOE_BLOCK).astype(I32)
    order = jnp.argsort(e_flat, stable=True).astype(I32)
    start = jnp.cumsum(counts) - counts
    slot_e = jnp.repeat(blk_expert, MOE_BLOCK)
    slot_r = jnp.arange(P, dtype=I32) - pstart[slot_e]
    slot_ok = slot_r < counts[slot_e]
    tok_of_slot = jnp.where(slot_ok, order[jnp.clip(start[slot_e] + slot_r, 0, A - 1)] // 2, 0).astype(I32)

    ys = _experts(hn3, tok_of_slot, blk_expert, n_used, w_e_gate.astype(BF16), w_e_up.astype(BF16),
                  w_e_down.astype(BF16))
    return h2, ys, slots, gts


def kernel(x, mem, norm_mix, w_in, shift_mu, rw_w0, rw_w2, rw_a0, rw_a2, rw_g2, rw_k_k, rw_k_a, rw_r_k, rw_ln_w, rw_ln_b, kv_norm, w_uk, w_uv, w_proj_a, w_proj_b, b_gate, w_out, norm_cross, norm_mem, w_cq, w_ckv, w_co, norm_ffn, w_router_g, b_router_g, w_router_e, b_router_e, w_e_gate, w_e_up, w_e_down, norm_final):
    B, S, D = x.shape
    depth = norm_mix.shape[0]
    h = x
    for l in range(depth):
        last = l == depth - 1
        h2, ys, slots, gts = _layer(
            h, mem, norm_mix[l], w_in[l], shift_mu[l], rw_w0[l], rw_w2[l], rw_a0[l], rw_a2[l],
            rw_g2[l], rw_k_k[l], rw_k_a[l], rw_r_k[l], rw_ln_w[l], rw_ln_b[l], kv_norm[l], w_uk[l],
            w_uv[l], w_proj_a[l], w_proj_b[l], b_gate[l], w_out[l], norm_cross[l], norm_mem[l],
            w_cq[l], w_ckv[l], w_co[l], norm_ffn[l], w_router_g[l], b_router_g[l], w_router_e[l],
            b_router_e[l], w_e_gate[l], w_e_up[l], w_e_down[l])
        h = _final(h2, ys, slots, gts, norm_final, last).reshape(B, S, D)
    return h
```
